```python
import jax
import jax.numpy as jnp
from jax import lax
import numpy as np

D_MODEL = 1024
BATCH = 16
SEQ = 4096
DEPTH = 4

CTX_LEN = 256
GRID_W = 64
N_MOD = 6
NORM_EPS = 1e-6
LN_EPS = 1e-5
RW_N = 64
RW_W = D_MODEL // 2
RW_H = RW_W // RW_N
LORA_DECAY = 64
LORA_ICLR = 64
LORA_GATE = 128
RW_GN_EPS = 64e-5
CONV_W = D_MODEL // 2
CONV_K = 31
ML_H = 4
ML_W = D_MODEL // 2
ML_DH = ML_W // ML_H
ML_CHUNK = 64
N_BRANCH = 3
D_FF = 4 * D_MODEL
RW_COLS = 3 * RW_W + 2 * LORA_DECAY + 2 * LORA_ICLR + LORA_GATE
CV_COLS = 2 * CONV_W
ML_COLS = 4 * ML_W + 4 * ML_H
GATE_COLS = N_BRANCH * D_MODEL
OFF_CV = RW_COLS
OFF_ML = OFF_CV + CV_COLS
OFF_GATE = OFF_ML + ML_COLS
P_TOTAL = OFF_GATE + GATE_COLS

kernel_name = 'hybrid_rwkv7_conformer_mlstm_block'


def _split_at(u, sizes):
    return jnp.split(u, [int(s) for s in np.cumsum(sizes)[:-1]], axis=-1)


def _rmsnorm(x, g):
    xf = x.astype(jnp.float32)
    y = xf * lax.rsqrt(jnp.mean(xf * xf, axis=-1, keepdims=True) + NORM_EPS)
    return (y * g.astype(jnp.float32)).astype(x.dtype)


def _layernorm(x, w, b, eps):
    xf = x.astype(jnp.float32)
    xc = xf - jnp.mean(xf, axis=-1, keepdims=True)
    y = xc * lax.rsqrt(jnp.mean(xc * xc, axis=-1, keepdims=True) + eps) * w.astype(jnp.float32)
    if b is not None:
        y = y + b.astype(jnp.float32)
    return y.astype(x.dtype)


def _token_shift(u, mu_prev, mu_next):
    prev = jnp.pad(u, ((0, 0), (1, 0), (0, 0)))[:, :-1]
    nxt = jnp.pad(u, ((0, 0), (0, 1), (0, 0)))[:, 1:]
    return u + mu_prev * (prev - u) + mu_next * (nxt - u)


def _rwkv_feats(u, w0, w2, a0, a2, k_k, k_a):
    B, T, _ = u.shape
    r, k, v, wd_f, wd_b, ad_f, ad_b, gd = _split_at(
        u, (RW_W, RW_W, RW_W, LORA_DECAY, LORA_DECAY, LORA_ICLR, LORA_ICLR, LORA_GATE))

    def heads(t):
        return t.astype(jnp.float32).reshape(B, T, RW_H, RW_N)

    kk = heads(k * k_k)
    kk = kk / jnp.maximum(jnp.sqrt(jnp.sum(kk * kk, axis=-1, keepdims=True)), 1e-12)
    dirs = []
    for d, (wd, ad) in enumerate(((wd_f, ad_f), (wd_b, ad_b))):
        w_log = -jax.nn.softplus(-(w0[d] + jnp.tanh(wd) @ w2[d]).astype(jnp.float32)) - 0.5
        decay = jnp.exp(-jnp.exp(w_log))
        a = jax.nn.sigmoid((a0[d] + ad @ a2[d]).astype(jnp.float32))
        k_d = k.astype(jnp.float32) * (1.0 + (a - 1.0) * k_a)
        dirs.append((heads(decay), heads(k_d), heads(a)))
    return heads(r), heads(k), heads(v), kk, dirs, gd


def _rwkv7_scan(s0, w, k, v, kk, a, r, reverse, emit):
    xs = tuple(jnp.moveaxis(t, 1, 0) for t in (w, k, v, kk, a))
    if emit:
        xs = xs + (jnp.moveaxis(r, 1, 0),)

    def step(S, inp):
        w_t, k_t, v_t, kk_t, a_t = inp[:5]
        S = (S * w_t[:, :, None, :]
             - jnp.einsum('bhvk,bhk->bhv', S, kk_t)[..., None] * (kk_t * a_t)[:, :, None, :]
             + v_t[..., None] * k_t[:, :, None, :])
        y = jnp.einsum('bhvk,bhk->bhv', S, inp[5]) if emit else None
        return S, y

    S, y = lax.scan(step, s0, xs, reverse=reverse)
    return S, (jnp.moveaxis(y, 0, 1) if emit else None)


def _rwkv_out(f, y, r_k, g2, gn_w, gn_b):
    r, k, v, _, _, gd = f
    B, T = r.shape[:2]
    yn = _layernorm(y, gn_w.reshape(RW_H, RW_N), gn_b.reshape(RW_H, RW_N), RW_GN_EPS)
    bonus = jnp.sum(r * k * r_k, axis=-1, keepdims=True) * v
    gate = jax.nn.sigmoid(gd) @ g2
    return (yn + bonus).reshape(B, T, RW_W) * gate


def _rwkv_mixer(u_lat, u_ctx, mu_p, mu_n, w0, w2, a0, a2, k_k, k_a, r_k, g2, gn_w, gn_b, emit_ctx):
    fl = _rwkv_feats(_token_shift(u_lat, mu_p, mu_n), w0, w2, a0, a2, k_k, k_a)
    fc = _rwkv_feats(_token_shift(u_ctx, mu_p, mu_n), w0, w2, a0, a2, k_k, k_a)
    s0 = jnp.zeros((u_lat.shape[0], RW_H, RW_N, RW_N), jnp.float32)
    y_lat, y_ctx = 0.0, 0.0
    for d in range(2):
        rev = d == 1
        w_c, k_c, a_c = fc[4][d]
        s_ctx, yc = _rwkv7_scan(s0, w_c, k_c, fc[2], fc[3], a_c, fc[0], rev, emit_ctx)
        w_l, k_l, a_l = fl[4][d]
        _, yl = _rwkv7_scan(s_ctx, w_l, k_l, fl[2], fl[3], a_l, fl[0], rev, True)
        y_lat = y_lat + yl
        if emit_ctx:
            y_ctx = y_ctx + yc
    out_lat = _rwkv_out(fl, y_lat, r_k, g2, gn_w, gn_b).astype(u_lat.dtype)
    out_ctx = _rwkv_out(fc, y_ctx, r_k, g2, gn_w, gn_b).astype(u_ctx.dtype) if emit_ctx else None
    return out_lat, out_ctx


def _depthwise_conv(z, w, b):
    y = lax.conv_general_dilated(z, w[:, None, :].astype(z.dtype), window_strides=(1,), padding='SAME',
                                 dimension_numbers=('NWC', 'WIO', 'NWC'), feature_group_count=z.shape[-1])
    return y + b


def _conv_branch(u, dw, db, ln_w, ln_b, rows):
    B, T, _ = u.shape
    a, g = jnp.split(u, 2, axis=-1)
    z = a * jax.nn.sigmoid(g)
    if rows is not None:
        z = _depthwise_conv(z.reshape(B * rows, GRID_W, CONV_W), dw, db).reshape(B, T, CONV_W)
    else:
        z = _depthwise_conv(z, dw, db)
    return jax.nn.silu(_layernorm(z, ln_w, ln_b, LN_EPS))


def _mlstm_feats(u, ib, fb):
    B, T, _ = u.shape
    q, k, v, o, gates = _split_at(u, (ML_W, ML_W, ML_W, ML_W, 4 * ML_H))

    def heads(t):
        return jnp.moveaxis(t.astype(jnp.float32).reshape(B, T, ML_H, ML_DH), 2, 1)

    gates = gates.astype(jnp.float32).reshape(B, T, 2, 2, ML_H)
    ig = jnp.moveaxis(gates[:, :, :, 0] + ib, 1, -1)
    lf = jnp.moveaxis(jax.nn.log_sigmoid(gates[:, :, :, 1] + fb), 1, -1)
    return heads(q), heads(k) * ML_DH ** -0.5, heads(v), o, ig, lf


def _mlstm_scan(state, k, v, ig, lf, q, emit):
    B, H, T, dh = k.shape
    nc = T // ML_CHUNK

    def chunks(t):
        return jnp.moveaxis(t.reshape((B, H, nc, ML_CHUNK) + t.shape[3:]), 2, 0)

    tri = jnp.tril(jnp.ones((ML_CHUNK, ML_CHUNK), dtype=bool))
    xs = (chunks(k), chunks(v), chunks(ig), chunks(lf)) + ((chunks(q),) if emit else ())

    def step(carry, inp):
        C, n, m = carry
        kc, vc, ic, fc = inp[:4]
        b = jnp.cumsum(fc, axis=-1)
        h = None
        if emit:
            qc = inp[4]
            dlog = jnp.where(tri, b[..., :, None] - b[..., None, :] + ic[..., None, :], -jnp.inf)
            inter = b + m[..., None]
            m_t = jnp.maximum(inter, jnp.max(dlog, axis=-1))
            s = jnp.einsum('bhtd,bhsd->bhts', qc, kc) * jnp.exp(dlog - m_t[..., None])
            w_inter = jnp.exp(inter - m_t)
            num = (w_inter[..., None] * jnp.einsum('bhvd,bhtd->bhtv', C, qc)
                   + jnp.einsum('bhts,bhsv->bhtv', s, vc))
            den = w_inter * jnp.einsum('bhd,bhtd->bht', n, qc) + jnp.sum(s, axis=-1)
            h = num / jnp.maximum(jnp.abs(den), jnp.exp(-m_t))[..., None]
        b_last = b[..., -1]
        wlog = b_last[..., None] - b + ic
        m_new = jnp.maximum(b_last + m, jnp.max(wlog, axis=-1))
        carry_w = jnp.exp(b_last + m - m_new)
        wk = jnp.exp(wlog - m_new[..., None])[..., None] * kc
        C_new = carry_w[..., None, None] * C + jnp.einsum('bhsv,bhsd->bhvd', vc, wk)
        n_new = carry_w[..., None] * n + jnp.sum(wk, axis=2)
        return (C_new, n_new, m_new), h

    state, h = lax.scan(step, state, xs)
    if emit:
        h = jnp.moveaxis(h, 0, 2).reshape(B, H, T, dh)
    return state, h


def _flip_time(t, rev):
    return jnp.flip(t, axis=2) if rev else t


def _mlstm_out(h, o, nw):
    B, H, T, dh = h.shape
    hn = _layernorm(jnp.moveaxis(h, 1, 2), nw.reshape(ML_H, ML_DH), None, LN_EPS)
    return hn.reshape(B, T, ML_W) * jax.nn.sigmoid(o.astype(jnp.float32))


def _mlstm_mixer(u_lat, u_ctx, ib, fb, nw, emit_ctx):
    ql, kl, vl, ol, il, fl = _mlstm_feats(u_lat, ib, fb)
    qc, kc, vc, oc, ic, fc = _mlstm_feats(u_ctx, ib, fb)
    B = u_lat.shape[0]
    init = (jnp.zeros((B, ML_H, ML_DH, ML_DH), jnp.float32), jnp.zeros((B, ML_H, ML_DH), jnp.float32),
            jnp.zeros((B, ML_H), jnp.float32))
    h_lat, h_ctx = 0.0, 0.0
    for d in range(2):
        rev = d == 1
        st, hc = _mlstm_scan(init, _flip_time(kc, rev), _flip_time(vc, rev), _flip_time(ic[:, d], rev),
                             _flip_time(fc[:, d], rev), _flip_time(qc, rev) if emit_ctx else None, emit_ctx)
        _, hl = _mlstm_scan(st, _flip_time(kl, rev), _flip_time(vl, rev), _flip_time(il[:, d], rev),
                            _flip_time(fl[:, d], rev), _flip_time(ql, rev), True)
        h_lat = h_lat + _flip_time(hl, rev)
        if emit_ctx:
            h_ctx = h_ctx + _flip_time(hc, rev)
    out_lat = _mlstm_out(h_lat, ol, nw).astype(u_lat.dtype)
    out_ctx = _mlstm_out(h_ctx, oc, nw).astype(u_ctx.dtype) if emit_ctx else None
    return out_lat, out_ctx


def _merge(u_gate, ya, yb, yc, pa, pb, pc, wo):
    ga, gb, gc = jnp.split(jax.nn.sigmoid(u_gate), N_BRANCH, axis=-1)
    m = ga * (ya @ pa) + gb * (yb @ pb) + gc * (yc @ pc)
    return m @ wo


def _mlp(h, w1, w2):
    return jnp.square(jax.nn.relu(h @ w1)) @ w2


def setup_inputs(seed: int = 0) -> dict:
    key = jax.random.key(seed)
    ks = iter(jax.random.split(key, 40))
    f32 = jnp.float32
    L, D = DEPTH, D_MODEL

    def nrm(shape, scale):
        return scale * jax.random.normal(next(ks), shape, f32)

    def unif(shape, lo, hi):
        return jax.random.uniform(next(ks), shape, f32, lo, hi)

    return {
        'x': nrm((BATCH, SEQ, D), 1.0),
        'c': nrm((BATCH, D), 1.0),
        'ctx': nrm((BATCH, CTX_LEN, D), 1.0),
        'c_ctx': nrm((D,), 1.0),
        'w_ada': nrm((L, D, N_MOD * D), 0.5 * D ** -0.5),
        'b_ada': nrm((L, N_MOD * D), 0.02),
        'g_norm1': 1.0 + nrm((L, D), 0.02),
        'g_norm2': 1.0 + nrm((L, D), 0.02),
        'w_in': nrm((L, D, P_TOTAL), D ** -0.5),
        'mu_prev': unif((L, RW_COLS), 0.0, 0.5),
        'mu_next': unif((L, RW_COLS), 0.0, 0.5),
        'rw_w0': unif((L, 2, RW_W), -6.0, 1.0),
        'rw_w2': nrm((L, 2, LORA_DECAY, RW_W), 0.1),
        'rw_a0': nrm((L, 2, RW_W), 0.5),
        'rw_a2': nrm((L, 2, LORA_ICLR, RW_W), LORA_ICLR ** -0.5),
        'rw_kk': 0.85 + nrm((L, RW_W), 0.05),
        'rw_ka': 1.0 + nrm((L, RW_W), 0.05),
        'rw_rk': nrm((L, RW_H, RW_N), 0.1),
        'rw_g2': nrm((L, LORA_GATE, RW_W), LORA_GATE ** -0.5),
        'rw_lnw': 1.0 + nrm((L, RW_W), 0.02),
        'rw_lnb': nrm((L, RW_W), 0.02),
        'cv_dw': nrm((L, CONV_K, CONV_W), CONV_K ** -0.5),
        'cv_db': nrm((L, CONV_W), 0.02),
        'cv_lnw': 1.0 + nrm((L, CONV_W), 0.02),
        'cv_lnb': nrm((L, CONV_W), 0.02),
        'ml_ib': nrm((L, 2, ML_H), 0.1),
        'ml_fb': jnp.linspace(3.0, 6.0, ML_H, dtype=f32) + nrm((L, 2, ML_H), 0.1),
        'ml_nw': 1.0 + nrm((L, ML_W), 0.02),
        'p_a': nrm((L, RW_W, D), RW_W ** -0.5),
        'p_b': nrm((L, CONV_W, D), CONV_W ** -0.5),
        'p_c': nrm((L, ML_W, D), ML_W ** -0.5),
        'w_out': nrm((L, D, D), D ** -0.5),
        'w_mlp1': nrm((L, D, D_FF), D ** -0.5),
        'w_mlp2': nrm((L, D_FF, D), D_FF ** -0.5),
        'g_final': 1.0 + nrm((D,), 0.02),
    }


def reference(x, c, ctx, c_ctx, w_ada, b_ada, g_norm1, g_norm2, w_in, mu_prev, mu_next,
              rw_w0, rw_w2, rw_a0, rw_a2, rw_kk, rw_ka, rw_rk, rw_g2, rw_lnw, rw_lnb,
              cv_dw, cv_db, cv_lnw, cv_lnb, ml_ib, ml_fb, ml_nw,
              p_a, p_b, p_c, w_out, w_mlp1, w_mlp2, g_final):
    rows = x.shape[1] // GRID_W
    for l in range(DEPTH):
        last = l == DEPTH - 1
        mod_x = jnp.split((jax.nn.silu(c) @ w_ada[l] + b_ada[l])[:, None, :], N_MOD, axis=-1)
        mod_c = jnp.split(jax.nn.silu(c_ctx) @ w_ada[l] + b_ada[l], N_MOD, axis=-1)
        h = _rmsnorm(x, g_norm1[l]) * (1.0 + mod_x[1]) + mod_x[0]
        hc = _rmsnorm(ctx, g_norm1[l]) * (1.0 + mod_c[1]) + mod_c[0]
        u_rw, u_cv, u_ml, u_gate = jnp.split(h @ w_in[l], [OFF_CV, OFF_ML, OFF_GATE], axis=-1)
        if last:
            uc_rw = hc @ w_in[l, :, :OFF_CV]
            uc_ml = hc @ w_in[l, :, OFF_ML:OFF_GATE]
        else:
            uc_rw, uc_cv, uc_ml, uc_gate = jnp.split(hc @ w_in[l], [OFF_CV, OFF_ML, OFF_GATE], axis=-1)
        ya, ya_c = _rwkv_mixer(u_rw, uc_rw, mu_prev[l], mu_next[l], rw_w0[l], rw_w2[l], rw_a0[l], rw_a2[l],
                               rw_kk[l], rw_ka[l], rw_rk[l], rw_g2[l], rw_lnw[l], rw_lnb[l], not last)
        yc, yc_c = _mlstm_mixer(u_ml, uc_ml, ml_ib[l], ml_fb[l], ml_nw[l], not last)
        yb = _conv_branch(u_cv, cv_dw[l], cv_db[l], cv_lnw[l], cv_lnb[l], rows)
        x = x + mod_x[2] * _merge(u_gate, ya, yb, yc, p_a[l], p_b[l], p_c[l], w_out[l])
        if not last:
            yb_c = _conv_branch(uc_cv, cv_dw[l], cv_db[l], cv_lnw[l], cv_lnb[l], None)
            ctx = ctx + mod_c[2] * _merge(uc_gate, ya_c, yb_c, yc_c, p_a[l], p_b[l], p_c[l], w_out[l])
        h = _rmsnorm(x, g_norm2[l]) * (1.0 + mod_x[4]) + mod_x[3]
        x = x + mod_x[5] * _mlp(h, w_mlp1[l], w_mlp2[l])
        if not last:
            hc = _rmsnorm(ctx, g_norm2[l]) * (1.0 + mod_c[4]) + mod_c[3]
            ctx = ctx + mod_c[5] * _mlp(hc, w_mlp1[l], w_mlp2[l])
    return _rmsnorm(x, g_final)
```

```python
import functools

import jax
import jax.numpy as jnp
from jax import lax
from jax.experimental import pallas as pl
from jax.experimental.pallas import tpu as pltpu

F32 = jnp.float32
BF16 = jnp.bfloat16

D_MODEL = 1024
GRID_W = 64
N_MOD = 6
NORM_EPS = 1e-6
LN_EPS = 1e-5
RW_N = 64
RW_W = D_MODEL // 2
RW_H = RW_W // RW_N
LORA_DECAY = 64
LORA_ICLR = 64
LORA_GATE = 128
RW_GN_EPS = 64e-5
CONV_W = D_MODEL // 2
CONV_K = 31
ML_H = 4
ML_W = D_MODEL // 2
ML_DH = ML_W // ML_H
N_BRANCH = 3
D_FF = 4 * D_MODEL
RW_COLS = 3 * RW_W + 2 * LORA_DECAY + 2 * LORA_ICLR + LORA_GATE
CV_COLS = 2 * CONV_W
ML_COLS = 4 * ML_W + 4 * ML_H
GATE_COLS = N_BRANCH * D_MODEL
OFF_CV = RW_COLS
OFF_ML = OFF_CV + CV_COLS
OFF_GATE = OFF_ML + ML_COLS

LANES = 128
SUBLANES = 8
TILE = 256
CHUNK = 64
HGROUP = 4
GW = HGROUP * RW_N
N_GROUPS = RW_H // HGROUP
MG_COLS = LANES
CONV_PAD = 16
VMEM_LIMIT = 56 * 1024 * 1024


def _mm(a, b):
    return jnp.dot(a.astype(BF16), b.astype(BF16), preferred_element_type=F32)


def _mm_nt(a, b):
    return lax.dot_general(a.astype(BF16), b.astype(BF16), (((1,), (1,)), ((), ())),
                           preferred_element_type=F32)


def _mm_tn(a, b):
    return lax.dot_general(a.astype(BF16), b.astype(BF16), (((0,), (0,)), ((), ())),
                           preferred_element_type=F32)


def _split3(x):
    hi = x.astype(BF16)
    r1 = x - hi.astype(F32)
    mid = r1.astype(BF16)
    lo = (r1 - mid.astype(F32)).astype(BF16)
    return hi, mid, lo


def _mm_exact_lhs(t, x):
    tb = t.astype(BF16)
    hi, mid, lo = _split3(x)
    dot = functools.partial(jnp.dot, preferred_element_type=F32)
    return dot(tb, hi) + dot(tb, mid) + dot(tb, lo)


def _mm_exact_rhs_nt(x, t):
    tb = t.astype(BF16)
    hi, mid, lo = _split3(x)
    dn = (((1,), (1,)), ((), ()))
    dot = functools.partial(lax.dot_general, dimension_numbers=dn, preferred_element_type=F32)
    return dot(hi, tb) + dot(mid, tb) + dot(lo, tb)


def _segsum(x, ones_bd):
    hi = x.astype(BF16)
    lo = (x - hi.astype(F32)).astype(BF16)
    dot = functools.partial(jnp.dot, preferred_element_type=F32)
    return dot(hi, ones_bd) + dot(lo, ones_bd)


def _sigmoid(x):
    return 1.0 / (1.0 + jnp.exp(-x))


def _softplus(x):
    return jnp.maximum(x, 0.0) + jnp.log(1.0 + jnp.exp(-jnp.abs(x)))


def _norm_mod(x, g, shift, scale):
    y = x * lax.rsqrt(jnp.mean(x * x, axis=-1, keepdims=True) + NORM_EPS)
    return (y * g) * (1.0 + scale) + shift


def _layernorm_lanes(x, eps):
    xc = x - jnp.mean(x, axis=-1, keepdims=True)
    return xc * lax.rsqrt(jnp.mean(xc * xc, axis=-1, keepdims=True) + eps)


def _const_spec(shape):
    nd = len(shape)
    return pl.BlockSpec(shape, lambda *_: (0,) * nd)


def _params(n_axes, arbitrary_last=False):
    sem = ["parallel"] * n_axes
    if arbitrary_last:
        sem[-1] = "arbitrary"
    return pltpu.CompilerParams(dimension_semantics=tuple(sem), vmem_limit_bytes=VMEM_LIMIT)


def _ada_kernel(c_ref, w_ref, b_ref, o_ref):
    c = c_ref[...]
    o_ref[0] = _mm(c * _sigmoid(c), w_ref[0]) + b_ref[0]


def _ada_mods(c_all, w_ada, b_ada):
    depth, d, nd = w_ada.shape
    rows = c_all.shape[0]
    return pl.pallas_call(
        _ada_kernel,
        grid=(depth, nd // d),
        in_specs=[pl.BlockSpec((rows, d), lambda l, j: (0, 0)),
                  pl.BlockSpec((1, d, d), lambda l, j: (l, 0, j)),
                  pl.BlockSpec((1, 1, d), lambda l, j: (l, 0, j))],
        out_specs=pl.BlockSpec((1, rows, d), lambda l, j: (l, 0, j)),
        out_shape=jax.ShapeDtypeStruct((depth, rows, nd), F32),
        compiler_params=_params(2),
        name="ada_mods",
    )(c_all, w_ada, b_ada.reshape(depth, 1, nd))


def _tile_spec(width):
    return pl.BlockSpec((1, TILE, width), lambda b, j: (b, j, 0))


def _mod_spec(ctx_row, n_ctx_tiles):
    return pl.BlockSpec((1, 1, N_MOD * D_MODEL),
                        lambda b, j: (jnp.where(j < n_ctx_tiles, ctx_row, b), 0, 0))


def _proj_kernel(x_ref, mod_ref, g_ref, w_ref, urw_ref, uml_ref, umg_ref):
    d = D_MODEL
    mod = mod_ref[0]
    h = _norm_mod(x_ref[0], g_ref[...], mod[:, 0:d], mod[:, d:2 * d])
    u = _mm(h, w_ref[...])
    urw_ref[0] = u[:, :RW_COLS]
    uml_ref[0] = u[:, RW_COLS:RW_COLS + 4 * ML_W]
    umg_ref[0] = u[:, RW_COLS + 4 * ML_W:]


def _proj_rec(xa, mod, g, w_rec, ctx_row, n_ctx_tiles):
    b, t, d = xa.shape
    ncol = w_rec.shape[1]
    return pl.pallas_call(
        _proj_kernel,
        grid=(b, t // TILE),
        in_specs=[_tile_spec(d), _mod_spec(ctx_row, n_ctx_tiles), _const_spec((1, d)),
                  _const_spec((d, ncol))],
        out_specs=[_tile_spec(RW_COLS), _tile_spec(4 * ML_W), _tile_spec(MG_COLS)],
        out_shape=[jax.ShapeDtypeStruct((b, t, RW_COLS), F32),
                   jax.ShapeDtypeStruct((b, t, 4 * ML_W), F32),
                   jax.ShapeDtypeStruct((b, t, MG_COLS), F32)],
        compiler_params=_params(2),
        name="proj_rec",
    )(xa, mod, g, w_rec)


def _chunk_index(s, rev, nc_ctx, nc_tot):
    if not rev:
        return s
    return jnp.where(s < nc_ctx, nc_ctx - 1 - s, nc_tot - 1 - (s - nc_ctx))


INV_BASE = 8


def _unit_tri_inverse(l_m, bd, eye, t_id, i_id):
    def same_block(n):
        sh = n.bit_length() - 1
        return lax.shift_right_logical(t_id, sh) == lax.shift_right_logical(i_id, sh)

    l_d = jnp.where(same_block(INV_BASE), l_m, 0.0)
    a = eye - l_d
    p2 = _mm(l_d, bd(l_d))
    t_m = a + _mm(a, bd(p2))
    p4 = _mm(p2, bd(p2))
    t_m = t_m + _mm(t_m, bd(p4))
    n = INV_BASE
    while n < CHUNK:
        off = jnp.where(jnp.logical_and(same_block(2 * n), jnp.logical_not(same_block(n))), l_m, 0.0)
        t_m = t_m - _mm(t_m, bd(_mm(off, bd(t_m))))
        n *= 2
    return t_m


def _rwkv_kernel(*refs, rev, finalize, nc_ctx, nc_tot):
    refs = list(refs)
    u_ref, up_ref, un_ref = refs[:3]
    pos = 3
    yo_ref = None
    if finalize:
        yo_ref = refs[pos]
        pos += 1
    (mup_ref, mun_ref, w0_ref, w2_ref, a0_ref, a2_ref, kk_ref, ka_ref, rk_ref, g2_ref,
     lnw_ref, lnb_ref, ones_ref, mbd_ref, out_ref, st_ref, buf_ref) = refs[pos:]

    s = pl.program_id(1)
    cidx = _chunk_index(s, rev, nc_ctx, nc_tot)

    @pl.when(s == 0)
    def _():
        st_ref[...] = jnp.zeros_like(st_ref)

    is_first = jnp.logical_or(cidx == 0, cidx == nc_ctx)
    is_last = jnp.logical_or(cidx == nc_ctx - 1, cidx == nc_tot - 1)
    u = u_ref[0]
    hs = SUBLANES
    buf_ref[hs:hs + CHUNK, :] = u
    buf_ref[hs - 1:hs, :] = jnp.where(is_first, 0.0, up_ref[0, hs - 1:hs, :])
    buf_ref[hs + CHUNK:hs + CHUNK + 1, :] = jnp.where(is_last, 0.0, un_ref[0, 0:1, :])
    prev = buf_ref[hs - 1:hs - 1 + CHUNK, :]
    nxt = buf_ref[hs + 1:hs + 1 + CHUNK, :]
    ush = u + mup_ref[...] * (prev - u) + mun_ref[...] * (nxt - u)

    w = RW_W
    r = ush[:, 0:w]
    k = ush[:, w:2 * w]
    v = ush[:, 2 * w:3 * w]
    wd = ush[:, 3 * w:3 * w + 2 * LORA_DECAY]
    ad = ush[:, 3 * w + 2 * LORA_DECAY:3 * w + 2 * LORA_DECAY + 2 * LORA_ICLR]
    ones_bd = ones_ref[...]

    w_log = -_softplus(-(w0_ref[...] + _mm(jnp.tanh(wd), w2_ref[...]))) - 0.5
    logw = -jnp.exp(w_log)
    a = _sigmoid(a0_ref[...] + _mm(ad, a2_ref[...]))
    kkr = k * kk_ref[...]
    kk = kkr / jnp.maximum(jnp.sqrt(_segsum(kkr * kkr, ones_bd)), 1e-12)
    k_d = k * (1.0 + (a - 1.0) * ka_ref[...])
    kka = kk * a

    row = lax.broadcasted_iota(jnp.int32, (CHUNK, CHUNK), 0)
    col = lax.broadcasted_iota(jnp.int32, (CHUNK, CHUNK), 1)
    tri = jnp.where((col >= row) if rev else (col <= row), 1.0, 0.0)
    c = _mm_exact_lhs(tri, logw)
    last = 0 if rev else CHUNK - 1
    c_tot = c[last:last + 1, :]
    e_neg = jnp.exp(-c)
    e_rem = jnp.exp(c_tot - c)
    alpha = kk * jnp.exp(c - logw)
    beta = kka * e_neg
    kappa = k_d * e_neg
    rho = r * jnp.exp(c)
    kap_hat = k_d * e_rem
    bet_hat = kka * e_rem
    w_tot = jnp.exp(c_tot)

    t_id = lax.broadcasted_iota(jnp.int32, (CHUNK, GW), 0)
    i_id = jnp.bitwise_and(lax.broadcasted_iota(jnp.int32, (CHUNK, GW), 1), RW_N - 1)
    strict = (i_id > t_id) if rev else (i_id < t_id)
    nonstrict = (i_id >= t_id) if rev else (i_id <= t_id)
    eye = jnp.where(i_id == t_id, 1.0, 0.0)
    mbd = mbd_ref[...]
    mbd_f = mbd.astype(F32)

    def bd(m):
        mb = m.astype(BF16)
        return jnp.concatenate([mb] * HGROUP, axis=0) * mbd

    ys = []
    for g in range(N_GROUPS):
        sl = slice(GW * g, GW * (g + 1))
        s0 = st_ref[g]
        vv = v[:, sl]
        ar = jnp.concatenate([alpha[:, sl], rho[:, sl]], axis=0)
        ab = _mm_nt(ar, bd(beta[:, sl]))
        ak = _mm_nt(ar, bd(kappa[:, sl]))
        a_s = _mm_nt(ar, bd(s0))
        l_m = jnp.where(strict, ab[:CHUNK], 0.0)
        p_b = jnp.where(nonstrict, ab[CHUNK:], 0.0)
        a_k = jnp.where(strict, ak[:CHUNK], 0.0)
        p_k = jnp.where(nonstrict, ak[CHUNK:], 0.0)
        akv = _mm(jnp.concatenate([a_k, p_k], axis=0), bd(vv))
        rhs = a_s[:CHUNK] + akv[:CHUNK]
        t_m = _unit_tri_inverse(l_m, bd, eye, t_id, i_id)
        uu = _mm(t_m, bd(rhs))
        ys.append(a_s[CHUNK:] + akv[CHUNK:] - _mm(p_b, bd(uu)))
        full = _mm_tn(jnp.concatenate([vv, uu], axis=0),
                      jnp.concatenate([kap_hat[:, sl], -bet_hat[:, sl]], axis=0)) * mbd_f
        s_new = s0 * w_tot[:, sl]
        for h in range(HGROUP):
            s_new = s_new + full[RW_N * h:RW_N * (h + 1)]
        st_ref[g] = s_new
    y = jnp.concatenate(ys, axis=1)

    if not finalize:
        out_ref[0] = y
        return

    y = y + yo_ref[0]
    mean = _segsum(y, ones_bd) * (1.0 / RW_N)
    yc = y - mean
    var = _segsum(yc * yc, ones_bd) * (1.0 / RW_N)
    yn = yc * lax.rsqrt(var + RW_GN_EPS) * lnw_ref[...] + lnb_ref[...]
    bonus = _segsum(r * k * rk_ref[...], ones_bd) * v
    gd = ush[:, 3 * w + 2 * LORA_DECAY + 2 * LORA_ICLR:]
    gate = _mm(_sigmoid(gd), g2_ref[...])
    out_ref[0] = (yn + bonus) * gate


def _rwkv_pass(u_rw, y_other, prm, rev, nc_ctx):
    b, t, _ = u_rw.shape
    nc_tot = t // CHUNK
    n8 = t // SUBLANES
    per8 = CHUNK // SUBLANES
    cidx = functools.partial(_chunk_index, rev=rev, nc_ctx=nc_ctx, nc_tot=nc_tot)
    finalize = y_other is not None
    in_specs = [
        pl.BlockSpec((1, CHUNK, RW_COLS), lambda i, s: (i, cidx(s), 0)),
        pl.BlockSpec((1, SUBLANES, RW_COLS), lambda i, s: (i, jnp.maximum(cidx(s) * per8 - 1, 0), 0)),
        pl.BlockSpec((1, SUBLANES, RW_COLS),
                     lambda i, s: (i, jnp.minimum(cidx(s) * per8 + per8, n8 - 1), 0)),
    ]
    args = [u_rw, u_rw, u_rw]
    if finalize:
        in_specs.append(pl.BlockSpec((1, CHUNK, RW_W), lambda i, s: (i, cidx(s), 0)))
        args.append(y_other)
    d = 1 if rev else 0
    consts = [prm["mu_p"], prm["mu_n"], prm["w0"][d], prm["w2"][d], prm["a0"][d], prm["a2"][d],
              prm["kk"], prm["ka"], prm["rk"], prm["g2"], prm["lnw"], prm["lnb"],
              prm["ones_bd"], prm["mask_bd"]]
    in_specs += [_const_spec(c.shape) for c in consts]
    return pl.pallas_call(
        functools.partial(_rwkv_kernel, rev=rev, finalize=finalize, nc_ctx=nc_ctx, nc_tot=nc_tot),
        grid=(b, nc_tot),
        in_specs=in_specs,
        out_specs=pl.BlockSpec((1, CHUNK, RW_W), lambda i, s: (i, cidx(s), 0)),
        out_shape=jax.ShapeDtypeStruct((b, t, RW_W), F32),
        scratch_shapes=[pltpu.VMEM((N_GROUPS, RW_N, GW), F32),
                        pltpu.VMEM((CHUNK + 2 * SUBLANES, RW_COLS), F32)],
        compiler_params=_params(2, arbitrary_last=True),
        name="rwkv_bwd" if rev else "rwkv_fwd",
    )(*args, *consts)


def _mlstm_kernel(*refs, rev, finalize, d):
    refs = list(refs)
    u_ref, g_ref = refs[:2]
    pos = 2
    ho_ref = None
    if finalize:
        ho_ref = refs[pos]
        pos += 1
    bias_ref, nw_ref, out_ref, c_ref, n_ref, m_ref = refs[pos:]
    s = pl.program_id(1)

    @pl.when(s == 0)
    def _():
        c_ref[...] = jnp.zeros_like(c_ref)
        n_ref[...] = jnp.zeros_like(n_ref)
        m_ref[...] = jnp.zeros_like(m_ref)

    u = u_ref[0]
    gp = g_ref[0] + bias_ref[...]
    lane = lax.broadcasted_iota(jnp.int32, (CHUNK, MG_COLS), 1)
    is_forget = jnp.bitwise_and(lane, ML_H) != 0
    gates = jnp.where(is_forget, -_softplus(-gp), gp)
    gates_t = gates.T

    row = lax.broadcasted_iota(jnp.int32, (CHUNK, CHUNK), 0)
    col = lax.broadcasted_iota(jnp.int32, (CHUNK, CHUNK), 1)
    seen = (col >= row) if rev else (col <= row)
    tri = jnp.where(seen, 1.0, 0.0)
    cum_col = _mm_exact_lhs(tri, gates)
    cum_row = _mm_exact_rhs_nt(gates_t, tri)
    last = 0 if rev else CHUNK - 1

    hs = []
    for h in range(ML_H):
        jf = 2 * ML_H * d + ML_H + h
        ji = 2 * ML_H * d + h
        hl = slice(ML_DH * h, ML_DH * (h + 1))
        q = u[:, hl]
        k = u[:, ML_W + ML_DH * h:ML_W + ML_DH * (h + 1)] * (ML_DH ** -0.5)
        v = u[:, 2 * ML_W + ML_DH * h:2 * ML_W + ML_DH * (h + 1)]
        b_col = cum_col[:, jf:jf + 1]
        b_row = cum_row[jf:jf + 1, :]
        i_col = gates[:, ji:ji + 1]
        i_row = gates_t[ji:ji + 1, :]
        c_st = c_ref[h]
        n_st = n_ref[h]
        m_st = m_ref[h][:, 0:1]

        dlog = jnp.where(seen, b_col - b_row + i_row, -jnp.inf)
        inter = b_col + m_st
        m_t = jnp.maximum(inter, jnp.max(dlog, axis=1, keepdims=True))
        s_mat = _mm_nt(q, k) * jnp.exp(dlog - m_t)
        w_inter = jnp.exp(inter - m_t)
        num = w_inter * _mm_nt(q, c_st) + _mm(s_mat, v)
        den = (w_inter * jnp.sum(q * n_st, axis=1, keepdims=True)
               + jnp.sum(s_mat, axis=1, keepdims=True))
        hs.append(num / jnp.maximum(jnp.abs(den), jnp.exp(-m_t)))

        b_last = b_col[last:last + 1, :]
        wlog = b_last - b_col + i_col
        m_new = jnp.maximum(b_last + m_st, jnp.max(wlog, axis=0, keepdims=True))
        carry_w = jnp.exp(b_last + m_st - m_new)
        wk = jnp.exp(wlog - m_new) * k
        c_ref[h] = carry_w * c_st + _mm_tn(v, wk)
        n_ref[h] = carry_w * n_st + jnp.sum(wk, axis=0, keepdims=True)
        m_ref[h] = jnp.broadcast_to(m_new, (1, LANES))
    hcat = jnp.concatenate(hs, axis=1)

    if not finalize:
        out_ref[0] = hcat
        return

    hcat = hcat + ho_ref[0]
    o = u[:, 3 * ML_W:]
    outs = []
    for h in range(ML_H):
        hl = slice(ML_DH * h, ML_DH * (h + 1))
        outs.append(_layernorm_lanes(hcat[:, hl], LN_EPS))
    out_ref[0] = jnp.concatenate(outs, axis=1) * nw_ref[...] * _sigmoid(o)


def _mlstm_pass(u_ml, u_mg, h_other, prm, rev, nc_ctx):
    b, t, _ = u_ml.shape
    nc_tot = t // CHUNK
    cidx = functools.partial(_chunk_index, rev=rev, nc_ctx=nc_ctx, nc_tot=nc_tot)
    finalize = h_other is not None
    in_specs = [pl.BlockSpec((1, CHUNK, 4 * ML_W), lambda i, s: (i, cidx(s), 0)),
                pl.BlockSpec((1, CHUNK, MG_COLS), lambda i, s: (i, cidx(s), 0))]
    args = [u_ml, u_mg]
    if finalize:
        in_specs.append(pl.BlockSpec((1, CHUNK, ML_W), lambda i, s: (i, cidx(s), 0)))
        args.append(h_other)
    consts = [prm["gate_bias"], prm["nw"]]
    in_specs += [_const_spec(c.shape) for c in consts]
    return pl.pallas_call(
        functools.partial(_mlstm_kernel, rev=rev, finalize=finalize, d=1 if rev else 0),
        grid=(b, nc_tot),
        in_specs=in_specs,
        out_specs=pl.BlockSpec((1, CHUNK, ML_W), lambda i, s: (i, cidx(s), 0)),
        out_shape=jax.ShapeDtypeStruct((b, t, ML_W), F32),
        scratch_shapes=[pltpu.VMEM((ML_H, ML_DH, ML_DH), F32),
                        pltpu.VMEM((ML_H, 1, ML_DH), F32),
                        pltpu.VMEM((ML_H, 1, LANES), F32)],
        compiler_params=_params(2, arbitrary_last=True),
        name="mlstm_bwd" if rev else "mlstm_fwd",
    )(*args, *consts)


def _conv_kernel(x_ref, mod_ref, g_ref, w_ref, dw_ref, db_ref, lnw_ref, lnb_ref, out_ref, pad_ref,
                 *, n_ctx_tiles):
    d = D_MODEL
    mod = mod_ref[0]
    h = _norm_mod(x_ref[0], g_ref[...], mod[:, 0:d], mod[:, d:2 * d])
    u = _mm(h, w_ref[...])
    z = u[:, :CONV_W] * _sigmoid(u[:, CONV_W:])
    dw = dw_ref[...]
    half = CONV_K // 2

    def conv_segments(seg_len):
        stride = seg_len + 2 * CONV_PAD
        zeros = jnp.zeros((CONV_PAD, CONV_W), F32)
        for i in range(TILE // seg_len):
            base = i * stride
            pad_ref[base:base + CONV_PAD, :] = zeros
            pad_ref[base + CONV_PAD:base + CONV_PAD + seg_len, :] = z[i * seg_len:(i + 1) * seg_len]
            pad_ref[base + CONV_PAD + seg_len:base + stride, :] = zeros
        for i in range(TILE // seg_len):
            base = i * stride + CONV_PAD - half
            acc = jnp.zeros((seg_len, CONV_W), F32)
            for j in range(CONV_K):
                acc = acc + dw[j:j + 1, :] * pad_ref[base + j:base + j + seg_len, :]
            y = _layernorm_lanes(acc + db_ref[...], LN_EPS) * lnw_ref[...] + lnb_ref[...]
            out_ref[0, i * seg_len:(i + 1) * seg_len, :] = y * _sigmoid(y)

    is_ctx = pl.program_id(1) < n_ctx_tiles

    @pl.when(is_ctx)
    def _():
        conv_segments(TILE)

    @pl.when(jnp.logical_not(is_ctx))
    def _():
        conv_segments(GRID_W)


def _conv_branch(xa, mod, g, prm, ctx_row, n_ctx_tiles):
    b, t, d = xa.shape
    consts = [g, prm["w_cv"], prm["dw"], prm["db"], prm["lnw"], prm["lnb"]]
    pad_rows = (TILE // GRID_W) * (GRID_W + 2 * CONV_PAD)
    return pl.pallas_call(
        functools.partial(_conv_kernel, n_ctx_tiles=n_ctx_tiles),
        grid=(b, t // TILE),
        in_specs=[_tile_spec(d), _mod_spec(ctx_row, n_ctx_tiles)] + [_const_spec(c.shape) for c in consts],
        out_specs=_tile_spec(CONV_W),
        out_shape=jax.ShapeDtypeStruct((b, t, CONV_W), F32),
        scratch_shapes=[pltpu.VMEM((pad_rows, CONV_W), F32)],
        compiler_params=_params(2),
        name="conv_branch",
    )(xa, mod, *consts)


def _merge_kernel(x_ref, mod_ref, g_ref, ya_ref, yb_ref, yc_ref, wg_ref, pa_ref, pb_ref, pc_ref, wo_ref,
                  out_ref):
    d = D_MODEL
    mod = mod_ref[0]
    x = x_ref[0]
    h = _norm_mod(x, g_ref[...], mod[:, 0:d], mod[:, d:2 * d])
    ug = _sigmoid(_mm(h, wg_ref[...]))
    m = (ug[:, 0:d] * _mm(ya_ref[0], pa_ref[...])
         + ug[:, d:2 * d] * _mm(yb_ref[0], pb_ref[...])
         + ug[:, 2 * d:3 * d] * _mm(yc_ref[0], pc_ref[...]))
    out_ref[0] = x + mod[:, 2 * d:3 * d] * _mm(m, wo_ref[...])


def _merge(xa, mod, g, ya, yb, yc, prm, ctx_row, n_ctx_tiles):
    b, t, d = xa.shape
    consts = [prm["w_gate"], prm["p_a"], prm["p_b"], prm["p_c"], prm["w_out"]]
    return pl.pallas_call(
        _merge_kernel,
        grid=(b, t // TILE),
        in_specs=[_tile_spec(d), _mod_spec(ctx_row, n_ctx_tiles), _const_spec(g.shape),
                  _tile_spec(RW_W), _tile_spec(CONV_W), _tile_spec(ML_W)]
        + [_const_spec(c.shape) for c in consts],
        out_specs=_tile_spec(d),
        out_shape=jax.ShapeDtypeStruct((b, t, d), F32),
        compiler_params=_params(2),
        name="merge",
    )(xa, mod, g, ya, yb, yc, *consts)


FF_CHUNK = 1024


def _mlp_kernel(x_ref, mod_ref, g_ref, w1_ref, w2_ref, out_ref):
    d = D_MODEL
    mod = mod_ref[0]
    x = x_ref[0]
    h = _norm_mod(x, g_ref[...], mod[:, 3 * d:4 * d], mod[:, 4 * d:5 * d]).astype(BF16)
    acc = jnp.zeros((TILE, d), F32)
    for i in range(D_FF // FF_CHUNK):
        hid = jnp.maximum(_mm(h, w1_ref[:, i * FF_CHUNK:(i + 1) * FF_CHUNK]), 0.0)
        acc = acc + _mm(hid * hid, w2_ref[i * FF_CHUNK:(i + 1) * FF_CHUNK, :])
    out_ref[0] = x + mod[:, 5 * d:6 * d] * acc


def _mlp(xa, mod, g, w1, w2, ctx_row, n_ctx_tiles):
    b, t, d = xa.shape
    return pl.pallas_call(
        _mlp_kernel,
        grid=(b, t // TILE),
        in_specs=[_tile_spec(d), _mod_spec(ctx_row, n_ctx_tiles), _const_spec(g.shape),
                  _const_spec(w1.shape), _const_spec(w2.shape)],
        out_specs=_tile_spec(d),
        out_shape=jax.ShapeDtypeStruct((b, t, d), F32),
        compiler_params=_params(2),
        name="mlp",
    )(xa, mod, g, w1, w2)


def _final_kernel(x_ref, g_ref, out_ref):
    x = x_ref[0]
    out_ref[0] = x * lax.rsqrt(jnp.mean(x * x, axis=-1, keepdims=True) + NORM_EPS) * g_ref[...]


def _final_norm(xa, g, n_ctx_tiles):
    b, t, d = xa.shape
    n_lat = t // TILE - n_ctx_tiles
    return pl.pallas_call(
        _final_kernel,
        grid=(b, n_lat),
        in_specs=[pl.BlockSpec((1, TILE, d), lambda i, j: (i, j + n_ctx_tiles, 0)), _const_spec(g.shape)],
        out_specs=pl.BlockSpec((1, TILE, d), lambda i, j: (i, j, 0)),
        out_shape=jax.ShapeDtypeStruct((b, n_lat * TILE, d), F32),
        compiler_params=_params(2),
        name="final_norm",
    )(xa, g)


def _pad_rows(m, rows, offset):
    out = jnp.zeros((rows, m.shape[1]), m.dtype)
    return out.at[offset:offset + m.shape[0]].set(m)


def kernel(x, c, ctx, c_ctx, w_ada, b_ada, g_norm1, g_norm2, w_in, mu_prev, mu_next,
           rw_w0, rw_w2, rw_a0, rw_a2, rw_kk, rw_ka, rw_rk, rw_g2, rw_lnw, rw_lnb,
           cv_dw, cv_db, cv_lnw, cv_lnb, ml_ib, ml_fb, ml_nw,
           p_a, p_b, p_c, w_out, w_mlp1, w_mlp2, g_final):
    batch, seq, d = x.shape
    ctx_len = ctx.shape[1]
    depth = w_ada.shape[0]
    assert d == D_MODEL and ctx_len == TILE and seq % TILE == 0
    n_ctx_tiles = ctx_len // TILE
    nc_ctx = ctx_len // CHUNK

    ctx_row = batch
    rows = -(-(batch + 1) // SUBLANES) * SUBLANES
    c_all = jnp.zeros((rows, d), F32).at[:batch].set(c).at[ctx_row].set(c_ctx)
    mods = _ada_mods(c_all, w_ada, b_ada)

    lane = jnp.arange(GW)
    mask_bd = (lane[:, None] // RW_N == lane[None, :] // RW_N).astype(BF16)
    lane = jnp.arange(RW_W)
    ones_bd = (lane[:, None] // RW_N == lane[None, :] // RW_N).astype(BF16)

    xa = jnp.concatenate([ctx, x], axis=1)
    for l in range(depth):
        mod = mods[l].reshape(rows, 1, N_MOD * d)
        g1 = g_norm1[l].reshape(1, d)
        g2 = g_norm2[l].reshape(1, d)
        wl = w_in[l]
        gate_w = _pad_rows(wl[:, OFF_ML + 4 * ML_W:OFF_GATE].T, MG_COLS, 0).T
        w_rec = jnp.concatenate([wl[:, :RW_COLS], wl[:, OFF_ML:OFF_ML + 4 * ML_W], gate_w],
                                axis=1).astype(BF16)
        u_rw, u_ml, u_mg = _proj_rec(xa, mod, g1, w_rec, ctx_row, n_ctx_tiles)

        rw = {
            "mu_p": mu_prev[l].reshape(1, RW_COLS), "mu_n": mu_next[l].reshape(1, RW_COLS),
            "w0": rw_w0[l].reshape(2, 1, RW_W), "a0": rw_a0[l].reshape(2, 1, RW_W),
            "w2": jnp.stack([_pad_rows(rw_w2[l, i], 2 * LORA_DECAY, LORA_DECAY * i)
                             for i in range(2)]).astype(BF16),
            "a2": jnp.stack([_pad_rows(rw_a2[l, i], 2 * LORA_ICLR, LORA_ICLR * i)
                             for i in range(2)]).astype(BF16),
            "kk": rw_kk[l].reshape(1, RW_W), "ka": rw_ka[l].reshape(1, RW_W),
            "rk": rw_rk[l].reshape(1, RW_W), "g2": rw_g2[l].astype(BF16),
            "lnw": rw_lnw[l].reshape(1, RW_W), "lnb": rw_lnb[l].reshape(1, RW_W),
            "ones_bd": ones_bd, "mask_bd": mask_bd,
        }
        y_b = _rwkv_pass(u_rw, None, rw, True, nc_ctx)
        ya = _rwkv_pass(u_rw, y_b, rw, False, nc_ctx)

        gate_bias = jnp.zeros((1, MG_COLS), F32).at[0, :4 * ML_H].set(
            jnp.stack([ml_ib[l], ml_fb[l]], axis=1).reshape(-1))
        ml = {"gate_bias": gate_bias, "nw": ml_nw[l].reshape(1, ML_W)}
        h_b = _mlstm_pass(u_ml, u_mg, None, ml, True, nc_ctx)
        yc = _mlstm_pass(u_ml, u_mg, h_b, ml, False, nc_ctx)

        cv = {"w_cv": wl[:, OFF_CV:OFF_ML].astype(BF16), "dw": cv_dw[l],
              "db": cv_db[l].reshape(1, CONV_W), "lnw": cv_lnw[l].reshape(1, CONV_W),
              "lnb": cv_lnb[l].reshape(1, CONV_W)}
        yb = _conv_branch(xa, mod, g1, cv, ctx_row, n_ctx_tiles)

        mg = {"w_gate": wl[:, OFF_GATE:].astype(BF16), "p_a": p_a[l].astype(BF16),
              "p_b": p_b[l].astype(BF16), "p_c": p_c[l].astype(BF16), "w_out": w_out[l].astype(BF16)}
        xa = _merge(xa, mod, g1, ya, yb, yc, mg, ctx_row, n_ctx_tiles)
        xa = _mlp(xa, mod, g2, w_mlp1[l].astype(BF16), w_mlp2[l].astype(BF16), ctx_row, n_ctx_tiles)
    return _final_norm(xa, g_final.reshape(1, d), n_ctx_tiles)
```

```python
import functools

import jax
import jax.numpy as jnp
from jax import lax
from jax.experimental import pallas as pl
from jax.experimental.pallas import tpu as pltpu

F32 = jnp.float32
BF16 = jnp.bfloat16

D_MODEL = 1024
GRID_W = 64
N_MOD = 6
NORM_EPS = 1e-6
LN_EPS = 1e-5
RW_N = 64
RW_W = D_MODEL // 2
RW_H = RW_W // RW_N
LORA_DECAY = 64
LORA_ICLR = 64
LORA_GATE = 128
RW_GN_EPS = 64e-5
CONV_W = D_MODEL // 2
CONV_K = 31
ML_H = 4
ML_W = D_MODEL // 2
ML_DH = ML_W // ML_H
N_BRANCH = 3
D_FF = 4 * D_MODEL
RW_COLS = 3 * RW_W + 2 * LORA_DECAY + 2 * LORA_ICLR + LORA_GATE
CV_COLS = 2 * CONV_W
ML_COLS = 4 * ML_W + 4 * ML_H
GATE_COLS = N_BRANCH * D_MODEL
OFF_CV = RW_COLS
OFF_ML = OFF_CV + CV_COLS
OFF_GATE = OFF_ML + ML_COLS

LANES = 128
SUBLANES = 8
TILE = 256
CHUNK = 64
HGROUP = 4
GW = HGROUP * RW_N
N_GROUPS = RW_H // HGROUP
MG_COLS = LANES
CONV_PAD = 16
VMEM_LIMIT = 56 * 1024 * 1024


def _mm(a, b):
    return jnp.dot(a.astype(BF16), b.astype(BF16), preferred_element_type=F32)


def _mm_nt(a, b):
    return lax.dot_general(a.astype(BF16), b.astype(BF16), (((1,), (1,)), ((), ())),
                           preferred_element_type=F32)


def _mm_tn(a, b):
    return lax.dot_general(a.astype(BF16), b.astype(BF16), (((0,), (0,)), ((), ())),
                           preferred_element_type=F32)


def _split3(x):
    hi = x.astype(BF16)
    r1 = x - hi.astype(F32)
    mid = r1.astype(BF16)
    lo = (r1 - mid.astype(F32)).astype(BF16)
    return hi, mid, lo


def _mm_exact_lhs(t, x):
    tb = t.astype(BF16)
    hi, mid, lo = _split3(x)
    dot = functools.partial(jnp.dot, preferred_element_type=F32)
    return dot(tb, hi) + dot(tb, mid) + dot(tb, lo)


def _mm_exact_rhs_nt(x, t):
    tb = t.astype(BF16)
    hi, mid, lo = _split3(x)
    dn = (((1,), (1,)), ((), ()))
    dot = functools.partial(lax.dot_general, dimension_numbers=dn, preferred_element_type=F32)
    return dot(hi, tb) + dot(mid, tb) + dot(lo, tb)


def _segsum(x, ones_bd):
    hi = x.astype(BF16)
    lo = (x - hi.astype(F32)).astype(BF16)
    dot = functools.partial(jnp.dot, preferred_element_type=F32)
    return dot(hi, ones_bd) + dot(lo, ones_bd)


def _sigmoid(x):
    return 1.0 / (1.0 + jnp.exp(-x))


def _softplus(x):
    return jnp.maximum(x, 0.0) + jnp.log(1.0 + jnp.exp(-jnp.abs(x)))


def _norm_mod(x, g, shift, scale):
    y = x * lax.rsqrt(jnp.mean(x * x, axis=-1, keepdims=True) + NORM_EPS)
    return (y * g) * (1.0 + scale) + shift


def _layernorm_lanes(x, eps):
    xc = x - jnp.mean(x, axis=-1, keepdims=True)
    return xc * lax.rsqrt(jnp.mean(xc * xc, axis=-1, keepdims=True) + eps)


def _const_spec(shape):
    nd = len(shape)
    return pl.BlockSpec(shape, lambda *_: (0,) * nd)


def _params(n_axes, arbitrary_last=False):
    sem = ["parallel"] * n_axes
    if arbitrary_last:
        sem[-1] = "arbitrary"
    return pltpu.CompilerParams(dimension_semantics=tuple(sem), vmem_limit_bytes=VMEM_LIMIT)


def _ada_kernel(c_ref, w_ref, b_ref, o_ref):
    c = c_ref[...]
    o_ref[0] = _mm(c * _sigmoid(c), w_ref[0]) + b_ref[0]


def _ada_mods(c_all, w_ada, b_ada):
    depth, d, nd = w_ada.shape
    rows = c_all.shape[0]
    return pl.pallas_call(
        _ada_kernel,
        grid=(depth, nd // d),
        in_specs=[pl.BlockSpec((rows, d), lambda l, j: (0, 0)),
                  pl.BlockSpec((1, d, d), lambda l, j: (l, 0, j)),
                  pl.BlockSpec((1, 1, d), lambda l, j: (l, 0, j))],
        out_specs=pl.BlockSpec((1, rows, d), lambda l, j: (l, 0, j)),
        out_shape=jax.ShapeDtypeStruct((depth, rows, nd), F32),
        compiler_params=_params(2),
        name="ada_mods",
    )(c_all, w_ada, b_ada.reshape(depth, 1, nd))


def _tile_spec(width):
    return pl.BlockSpec((1, TILE, width), lambda b, j: (b, j, 0))


def _mod_spec(ctx_row, n_ctx_tiles):
    return pl.BlockSpec((1, 1, N_MOD * D_MODEL),
                        lambda b, j: (jnp.where(j < n_ctx_tiles, ctx_row, b), 0, 0))


def _proj_kernel(x_ref, mod_ref, g_ref, w_ref, urw_ref, uml_ref, umg_ref):
    d = D_MODEL
    mod = mod_ref[0]
    h = _norm_mod(x_ref[0], g_ref[...], mod[:, 0:d], mod[:, d:2 * d])
    u = _mm(h, w_ref[...])
    urw_ref[0] = u[:, :RW_COLS]
    uml_ref[0] = u[:, RW_COLS:RW_COLS + 4 * ML_W]
    umg_ref[0] = u[:, RW_COLS + 4 * ML_W:]


def _proj_rec(xa, mod, g, w_rec, ctx_row, n_ctx_tiles):
    b, t, d = xa.shape
    ncol = w_rec.shape[1]
    return pl.pallas_call(
        _proj_kernel,
        grid=(b, t // TILE),
        in_specs=[_tile_spec(d), _mod_spec(ctx_row, n_ctx_tiles), _const_spec((1, d)),
                  _const_spec((d, ncol))],
        out_specs=[_tile_spec(RW_COLS), _tile_spec(4 * ML_W), _tile_spec(MG_COLS)],
        out_shape=[jax.ShapeDtypeStruct((b, t, RW_COLS), F32),
                   jax.ShapeDtypeStruct((b, t, 4 * ML_W), F32),
                   jax.ShapeDtypeStruct((b, t, MG_COLS), F32)],
        compiler_params=_params(2),
        name="proj_rec",
    )(xa, mod, g, w_rec)


def _chunk_index(s, rev, nc_ctx, nc_tot):
    if not rev:
        return s
    return jnp.where(s < nc_ctx, nc_ctx - 1 - s, nc_tot - 1 - (s - nc_ctx))


INV_BASE = 8


def _unit_tri_inverse(l_ms, bd, eye, t_id, i_id):
    def same_block(n):
        sh = n.bit_length() - 1
        return lax.shift_right_logical(t_id, sh) == lax.shift_right_logical(i_id, sh)

    diag = same_block(INV_BASE)
    l_d = [jnp.where(diag, l_m, 0.0) for l_m in l_ms]
    a = [eye - m for m in l_d]
    p2 = [_mm(m, bd(m)) for m in l_d]
    t_m = [x + _mm(x, bd(p)) for x, p in zip(a, p2)]
    p4 = [_mm(p, bd(p)) for p in p2]
    t_m = [t + _mm(t, bd(p)) for t, p in zip(t_m, p4)]
    n = INV_BASE
    while n < CHUNK:
        sub = jnp.logical_and(same_block(2 * n), jnp.logical_not(same_block(n)))
        g_m = [_mm(jnp.where(sub, l_m, 0.0), bd(t)) for l_m, t in zip(l_ms, t_m)]
        t_m = [t - _mm(t, bd(g)) for t, g in zip(t_m, g_m)]
        n *= 2
    return t_m


RW_TILE = 2 * CHUNK
RW_SCAN_BATCH = 8


def _shifted_tile(u_ref, up_ref, un_ref, buf_ref, mu_p, mu_n, is_first, is_last, rows):
    u = u_ref[0]
    hs = SUBLANES
    buf_ref[hs:hs + rows, :] = u
    buf_ref[hs - 1:hs, :] = jnp.where(is_first, 0.0, up_ref[0, hs - 1:hs, :])
    buf_ref[hs + rows:hs + rows + 1, :] = jnp.where(is_last, 0.0, un_ref[0, 0:1, :])
    prev = buf_ref[hs - 1:hs - 1 + rows, :]
    nxt = buf_ref[hs + 1:hs + 1 + rows, :]
    return u + mu_p * (prev - u) + mu_n * (nxt - u)


def _halo_specs(rows, n8):
    per8 = rows // SUBLANES
    return [pl.BlockSpec((1, rows, RW_COLS), lambda i, j: (i, j, 0)),
            pl.BlockSpec((1, SUBLANES, RW_COLS), lambda i, j: (i, jnp.maximum(j * per8 - 1, 0), 0)),
            pl.BlockSpec((1, SUBLANES, RW_COLS), lambda i, j: (i, jnp.minimum((j + 1) * per8, n8 - 1), 0))]


def _rwkv_pre_kernel(u_ref, up_ref, un_ref, mup_ref, mun_ref, w0_ref, w2_ref, a0_ref, a2_ref, kk_ref, ka_ref,
                     ones_ref, mbd_ref, y0_ref, rt_ref, gm_ref, dm_ref, wt_ref, buf_ref, *, ctx_tiles, n_tiles):
    j = pl.program_id(1)
    is_first = jnp.logical_or(j == 0, j == ctx_tiles)
    is_last = jnp.logical_or(j == ctx_tiles - 1, j == n_tiles - 1)
    ush = _shifted_tile(u_ref, up_ref, un_ref, buf_ref, mup_ref[...], mun_ref[...], is_first, is_last, RW_TILE)

    w = RW_W
    r = ush[:, 0:w]
    k = ush[:, w:2 * w]
    v = ush[:, 2 * w:3 * w]
    wd = jnp.tanh(ush[:, 3 * w:3 * w + 2 * LORA_DECAY])
    ad = ush[:, 3 * w + 2 * LORA_DECAY:3 * w + 2 * LORA_DECAY + 2 * LORA_ICLR]
    kkr = k * kk_ref[...]
    kk = kkr / jnp.maximum(jnp.sqrt(_segsum(kkr * kkr, ones_ref[...])), 1e-12)

    row = lax.broadcasted_iota(jnp.int32, (CHUNK, CHUNK), 0)
    col = lax.broadcasted_iota(jnp.int32, (CHUNK, CHUNK), 1)
    t_id = lax.broadcasted_iota(jnp.int32, (CHUNK, GW), 0)
    i_id = jnp.bitwise_and(lax.broadcasted_iota(jnp.int32, (CHUNK, GW), 1), RW_N - 1)
    eye = jnp.where(i_id == t_id, 1.0, 0.0)
    mbd = mbd_ref[...]
    mbd_f2 = jnp.concatenate([mbd, mbd], axis=1).astype(F32)
    zeros = jnp.zeros((CHUNK, GW), F32)

    def bd(m):
        mb = m.astype(BF16)
        return jnp.concatenate([mb] * HGROUP, axis=0) * mbd

    logw_d, k_d_d, kka_d = [], [], []
    for d in range(2):
        w_log = -_softplus(-(w0_ref[d] + _mm(wd, w2_ref[d]))) - 0.5
        logw_d.append(-jnp.exp(w_log))
        a = _sigmoid(a0_ref[d] + _mm(ad, a2_ref[d]))
        k_d_d.append(k * (1.0 + (a - 1.0) * ka_ref[...]))
        kka_d.append(kk * a)
    units = [(d, ci) for d in range(2) for ci in range(RW_TILE // CHUNK)]
    rows_of = lambda ci: slice(ci * CHUNK, (ci + 1) * CHUNK)
    tris = [jnp.where((col >= row) if d == 1 else (col <= row), 1.0, 0.0) for d in range(2)]
    cums = [_mm_exact_lhs(tris[d], logw_d[d][rows_of(ci)]) for d, ci in units]

    chains = []
    for (d, ci), c in zip(units, cums):
        rev = d == 1
        rows = rows_of(ci)
        logw = logw_d[d][rows]
        k_d = k_d_d[d][rows]
        kka = kka_d[d][rows]
        last = 0 if rev else CHUNK - 1
        c_tot = c[last:last + 1, :]
        e_neg = jnp.exp(-c)
        e_rem = jnp.exp(c_tot - c)
        alpha = kk[rows] * jnp.exp(c - logw)
        beta = kka * e_neg
        kappa = k_d * e_neg
        rho = r[rows] * jnp.exp(c)
        kap_hat = k_d * e_rem
        bet_hat = kka * e_rem
        wt_ref[d, 0, ci] = jnp.exp(c_tot)
        strict = (i_id > t_id) if rev else (i_id < t_id)
        nonstrict = (i_id >= t_id) if rev else (i_id <= t_id)
        for g in range(N_GROUPS):
            sl = slice(GW * g, GW * (g + 1))
            chains.append(dict(alpha=alpha[:, sl], beta=beta[:, sl], kappa=kappa[:, sl], rho=rho[:, sl],
                               kap_hat=kap_hat[:, sl], bet_hat=bet_hat[:, sl], v=v[rows, sl],
                               strict=strict, nonstrict=nonstrict))

    ar = [jnp.concatenate([ch["alpha"], ch["rho"]], axis=0) for ch in chains]
    ab = [_mm_nt(x, bd(ch["beta"])) for x, ch in zip(ar, chains)]
    ak = [_mm_nt(x, bd(ch["kappa"])) for x, ch in zip(ar, chains)]
    l_m = [jnp.where(ch["strict"], x[:CHUNK], 0.0) for x, ch in zip(ab, chains)]
    p_b = [jnp.where(ch["nonstrict"], x[CHUNK:], 0.0) for x, ch in zip(ab, chains)]
    akm = [jnp.concatenate([jnp.where(ch["strict"], x[:CHUNK], 0.0),
                            jnp.where(ch["nonstrict"], x[CHUNK:], 0.0)], axis=0) for x, ch in zip(ak, chains)]
    akv = [_mm(x, bd(ch["v"])) for x, ch in zip(akm, chains)]
    t_m = _unit_tri_inverse(l_m, bd, eye, t_id, i_id)
    tu = [_mm(t, jnp.concatenate([bd(x[:CHUNK]), bd(ch["alpha"])], axis=1)) for t, x, ch in zip(t_m, akv, chains)]
    pb = [_mm(p, jnp.concatenate([bd(x[:, :GW]), bd(x[:, GW:])], axis=1)) for p, x in zip(p_b, tu)]
    y0 = [x[CHUNK:] - p[:, :GW] for x, p in zip(akv, pb)]
    rt = [ch["rho"] - p[:, GW:] for ch, p in zip(chains, pb)]
    full = [_mm_tn(jnp.concatenate([ch["v"], x[:, :GW], x[:, GW:]], axis=0),
                   jnp.concatenate(
                       [jnp.concatenate([ch["kap_hat"], -ch["bet_hat"], zeros], axis=0),
                        jnp.concatenate([zeros, zeros, ch["bet_hat"]], axis=0)], axis=1)) * mbd_f2
            for ch, x in zip(chains, tu)]
    blk = [sum(x[RW_N * h:RW_N * (h + 1)] for h in range(1, HGROUP)) + x[0:RW_N] for x in full]

    for ui, (d, ci) in enumerate(units):
        rows = rows_of(ci)
        sel = range(ui * N_GROUPS, (ui + 1) * N_GROUPS)
        y0_ref[d, 0, rows, :] = jnp.concatenate([y0[i] for i in sel], axis=1)
        rt_ref[d, 0, rows, :] = jnp.concatenate([rt[i] for i in sel], axis=1).astype(BF16)
        gm_ref[d, 0, rows, :] = jnp.concatenate([blk[i][:, GW:] for i in sel], axis=1).astype(BF16)
        dm_ref[d, 0, rows, :] = jnp.concatenate([blk[i][:, :GW] for i in sel], axis=1)


def _rwkv_pre(u_rw, prm, ctx_len):
    b, t, _ = u_rw.shape
    n_tiles = t // RW_TILE
    nc = t // CHUNK
    cpt = RW_TILE // CHUNK
    consts = [prm["mu_p"], prm["mu_n"], prm["w0"], prm["w2"], prm["a0"], prm["a2"], prm["kk"], prm["ka"],
              prm["ones_bd"], prm["mask_bd"]]
    tile_out = pl.BlockSpec((2, 1, RW_TILE, RW_W), lambda i, j: (0, i, j, 0))
    return pl.pallas_call(
        functools.partial(_rwkv_pre_kernel, ctx_tiles=ctx_len // RW_TILE, n_tiles=n_tiles),
        grid=(b, n_tiles),
        in_specs=_halo_specs(RW_TILE, t // SUBLANES) + [_const_spec(c.shape) for c in consts],
        out_specs=[tile_out, tile_out, tile_out, tile_out,
                   pl.BlockSpec((2, 1, cpt, 1, RW_W), lambda i, j: (0, i, j, 0, 0))],
        out_shape=[jax.ShapeDtypeStruct((2, b, t, RW_W), F32),
                   jax.ShapeDtypeStruct((2, b, t, RW_W), BF16),
                   jax.ShapeDtypeStruct((2, b, t, RW_W), BF16),
                   jax.ShapeDtypeStruct((2, b, t, RW_W), F32),
                   jax.ShapeDtypeStruct((2, b, nc, 1, RW_W), F32)],
        scratch_shapes=[pltpu.VMEM((RW_TILE + 2 * SUBLANES, RW_COLS), F32)],
        compiler_params=_params(2),
        name="rwkv_pre",
    )(u_rw, u_rw, u_rw, *consts)


def _rwkv_scan_kernel(y0_ref, rt_ref, gm_ref, dm_ref, wt_ref, mbd_ref, y_ref, st_ref, *, nb):
    @pl.when(pl.program_id(2) == 0)
    def _():
        st_ref[...] = jnp.zeros_like(st_ref)

    mbd = mbd_ref[...]

    def bd(mb):
        return jnp.concatenate([mb] * HGROUP, axis=0) * mbd

    for bi in range(nb):
        ys = []
        for g in range(N_GROUPS):
            sl = slice(GW * g, GW * (g + 1))
            s0 = st_ref[bi, g]
            s0b = s0.astype(BF16)
            ys.append(y0_ref[0, bi, :, sl] + lax.dot_general(
                rt_ref[0, bi, :, sl], bd(s0b), (((1,), (1,)), ((), ())), preferred_element_type=F32))
            st_ref[bi, g] = (s0 * wt_ref[0, bi, 0, :, sl]
                             - jnp.dot(s0b, bd(gm_ref[0, bi, :, sl]), preferred_element_type=F32)
                             + dm_ref[0, bi, :, sl])
        y_ref[0, bi] = jnp.concatenate(ys, axis=1)


def _rwkv_scan(y0, rt, gm, dm, wt, mask_bd, nc_ctx):
    _, b, t, _ = y0.shape
    nc_tot = t // CHUNK
    nb = RW_SCAN_BATCH if b % RW_SCAN_BATCH == 0 else 1

    def cidx(d, s):
        rev = jnp.where(s < nc_ctx, nc_ctx - 1 - s, nc_tot - 1 - (s - nc_ctx))
        return jnp.where(d == 1, rev, s)

    chunk = pl.BlockSpec((1, nb, CHUNK, RW_W), lambda d, i, s: (d, i, cidx(d, s), 0))
    return pl.pallas_call(
        functools.partial(_rwkv_scan_kernel, nb=nb),
        grid=(2, b // nb, nc_tot),
        in_specs=[chunk, chunk, chunk, chunk,
                  pl.BlockSpec((1, nb, 1, 1, RW_W), lambda d, i, s: (d, i, cidx(d, s), 0, 0)),
                  _const_spec(mask_bd.shape)],
        out_specs=chunk,
        out_shape=jax.ShapeDtypeStruct(y0.shape, F32),
        scratch_shapes=[pltpu.VMEM((nb, N_GROUPS, RW_N, GW), F32)],
        compiler_params=_params(3, arbitrary_last=True),
        name="rwkv_scan",
    )(y0, rt, gm, dm, wt, mask_bd)


def _rwkv_out_kernel(u_ref, up_ref, un_ref, y_ref, mup_ref, mun_ref, rk_ref, g2_ref, lnw_ref, lnb_ref, ones_ref,
                     out_ref, buf_ref, *, ctx_tiles, n_tiles):
    j = pl.program_id(1)
    is_first = jnp.logical_or(j == 0, j == ctx_tiles)
    is_last = jnp.logical_or(j == ctx_tiles - 1, j == n_tiles - 1)
    ush = _shifted_tile(u_ref, up_ref, un_ref, buf_ref, mup_ref[...], mun_ref[...], is_first, is_last, TILE)
    w = RW_W
    r = ush[:, 0:w]
    k = ush[:, w:2 * w]
    v = ush[:, 2 * w:3 * w]
    gd = ush[:, 3 * w + 2 * LORA_DECAY + 2 * LORA_ICLR:]
    ones_bd = ones_ref[...]
    y = y_ref[0, 0] + y_ref[1, 0]
    mean = _segsum(y, ones_bd) * (1.0 / RW_N)
    yc = y - mean
    var = _segsum(yc * yc, ones_bd) * (1.0 / RW_N)
    yn = yc * lax.rsqrt(var + RW_GN_EPS) * lnw_ref[...] + lnb_ref[...]
    bonus = _segsum(r * k * rk_ref[...], ones_bd) * v
    out_ref[0] = (yn + bonus) * _mm(_sigmoid(gd), g2_ref[...])


def _rwkv_out(u_rw, y, prm, ctx_len):
    b, t, _ = u_rw.shape
    n_tiles = t // TILE
    consts = [prm["mu_p"], prm["mu_n"], prm["rk"], prm["g2"], prm["lnw"], prm["lnb"], prm["ones_bd"]]
    return pl.pallas_call(
        functools.partial(_rwkv_out_kernel, ctx_tiles=ctx_len // TILE, n_tiles=n_tiles),
        grid=(b, n_tiles),
        in_specs=_halo_specs(TILE, t // SUBLANES)
        + [pl.BlockSpec((2, 1, TILE, RW_W), lambda i, j: (0, i, j, 0))]
        + [_const_spec(c.shape) for c in consts],
        out_specs=_tile_spec(RW_W),
        out_shape=jax.ShapeDtypeStruct((b, t, RW_W), F32),
        scratch_shapes=[pltpu.VMEM((TILE + 2 * SUBLANES, RW_COLS), F32)],
        compiler_params=_params(2),
        name="rwkv_out",
    )(u_rw, u_rw, u_rw, y, *consts)


def _rwkv_mixer(u_rw, prm, ctx_len):
    y0, rt, gm, dm, wt = _rwkv_pre(u_rw, prm, ctx_len)
    y = _rwkv_scan(y0, rt, gm, dm, wt, prm["mask_bd"], ctx_len // CHUNK)
    return _rwkv_out(u_rw, y, prm, ctx_len)


ML_BATCH = 4


def _mlstm_kernel(*refs, rev, finalize, d, nb):
    refs = list(refs)
    u_ref, g_ref = refs[:2]
    pos = 2
    ho_ref = None
    if finalize:
        ho_ref = refs[pos]
        pos += 1
    bias_ref, nw_ref, out_ref, c_ref, n_ref, m_ref = refs[pos:]
    s = pl.program_id(1)

    @pl.when(s == 0)
    def _():
        c_ref[...] = jnp.zeros_like(c_ref)
        n_ref[...] = jnp.zeros_like(n_ref)
        m_ref[...] = jnp.zeros_like(m_ref)

    lane = lax.broadcasted_iota(jnp.int32, (CHUNK, MG_COLS), 1)
    is_forget = jnp.bitwise_and(lane, ML_H) != 0
    row = lax.broadcasted_iota(jnp.int32, (CHUNK, CHUNK), 0)
    col = lax.broadcasted_iota(jnp.int32, (CHUNK, CHUNK), 1)
    seen = (col >= row) if rev else (col <= row)
    tri = jnp.where(seen, 1.0, 0.0)
    last = 0 if rev else CHUNK - 1

    us = [u_ref[bi] for bi in range(nb)]
    gps = [g_ref[bi] + bias_ref[...] for bi in range(nb)]
    gates = [jnp.where(is_forget, -_softplus(-gp), gp) for gp in gps]
    gates_t = [g.T for g in gates]
    cum_col = [_mm_exact_lhs(tri, g) for g in gates]
    cum_row = [_mm_exact_rhs_nt(g, tri) for g in gates_t]

    chains = [(bi, h) for bi in range(nb) for h in range(ML_H)]
    jf = lambda h: 2 * ML_H * d + ML_H + h
    ji = lambda h: 2 * ML_H * d + h
    q = [us[bi][:, ML_DH * h:ML_DH * (h + 1)] for bi, h in chains]
    k = [us[bi][:, ML_W + ML_DH * h:ML_W + ML_DH * (h + 1)] * (ML_DH ** -0.5) for bi, h in chains]
    v = [us[bi][:, 2 * ML_W + ML_DH * h:2 * ML_W + ML_DH * (h + 1)] for bi, h in chains]
    b_col = [cum_col[bi][:, jf(h):jf(h) + 1] for bi, h in chains]
    b_row = [cum_row[bi][jf(h):jf(h) + 1, :] for bi, h in chains]
    i_col = [gates[bi][:, ji(h):ji(h) + 1] for bi, h in chains]
    i_row = [gates_t[bi][ji(h):ji(h) + 1, :] for bi, h in chains]
    c_st = [c_ref[bi * ML_H + h] for bi, h in chains]
    n_st = [n_ref[bi * ML_H + h] for bi, h in chains]
    m_st = [m_ref[bi * ML_H + h][:, 0:1] for bi, h in chains]
    n_ch = range(len(chains))

    qk = [_mm_nt(q[i], k[i]) for i in n_ch]
    qc = [_mm_nt(q[i], c_st[i]) for i in n_ch]
    dlog = [jnp.where(seen, b_col[i] - b_row[i] + i_row[i], -jnp.inf) for i in n_ch]
    inter = [b_col[i] + m_st[i] for i in n_ch]
    m_t = [jnp.maximum(inter[i], jnp.max(dlog[i], axis=1, keepdims=True)) for i in n_ch]
    s_mat = [qk[i] * jnp.exp(dlog[i] - m_t[i]) for i in n_ch]
    sv = [_mm(s_mat[i], v[i]) for i in n_ch]
    b_last = [b_col[i][last:last + 1, :] for i in n_ch]
    wlog = [b_last[i] - b_col[i] + i_col[i] for i in n_ch]
    m_new = [jnp.maximum(b_last[i] + m_st[i], jnp.max(wlog[i], axis=0, keepdims=True)) for i in n_ch]
    wk = [jnp.exp(wlog[i] - m_new[i]) * k[i] for i in n_ch]
    vwk = [_mm_tn(v[i], wk[i]) for i in n_ch]

    hs = []
    for i, (bi, h) in enumerate(chains):
        w_inter = jnp.exp(inter[i] - m_t[i])
        num = w_inter * qc[i] + sv[i]
        den = (w_inter * jnp.sum(q[i] * n_st[i], axis=1, keepdims=True)
               + jnp.sum(s_mat[i], axis=1, keepdims=True))
        hs.append(num / jnp.maximum(jnp.abs(den), jnp.exp(-m_t[i])))
        carry_w = jnp.exp(b_last[i] + m_st[i] - m_new[i])
        st = bi * ML_H + h
        c_ref[st] = carry_w * c_st[i] + vwk[i]
        n_ref[st] = carry_w * n_st[i] + jnp.sum(wk[i], axis=0, keepdims=True)
        m_ref[st] = jnp.broadcast_to(m_new[i], (1, LANES))

    for bi in range(nb):
        hcat = jnp.concatenate(hs[bi * ML_H:(bi + 1) * ML_H], axis=1)
        if not finalize:
            out_ref[bi] = hcat
            continue
        hcat = hcat + ho_ref[bi]
        o = us[bi][:, 3 * ML_W:]
        outs = [_layernorm_lanes(hcat[:, ML_DH * h:ML_DH * (h + 1)], LN_EPS) for h in range(ML_H)]
        out_ref[bi] = jnp.concatenate(outs, axis=1) * nw_ref[...] * _sigmoid(o)


def _mlstm_pass(u_ml, u_mg, h_other, prm, rev, nc_ctx):
    b, t, _ = u_ml.shape
    nc_tot = t // CHUNK
    nb = ML_BATCH if b % ML_BATCH == 0 else 1
    cidx = functools.partial(_chunk_index, rev=rev, nc_ctx=nc_ctx, nc_tot=nc_tot)
    finalize = h_other is not None
    in_specs = [pl.BlockSpec((nb, CHUNK, 4 * ML_W), lambda i, s: (i, cidx(s), 0)),
                pl.BlockSpec((nb, CHUNK, MG_COLS), lambda i, s: (i, cidx(s), 0))]
    args = [u_ml, u_mg]
    if finalize:
        in_specs.append(pl.BlockSpec((nb, CHUNK, ML_W), lambda i, s: (i, cidx(s), 0)))
        args.append(h_other)
    consts = [prm["gate_bias"], prm["nw"]]
    in_specs += [_const_spec(c.shape) for c in consts]
    return pl.pallas_call(
        functools.partial(_mlstm_kernel, rev=rev, finalize=finalize, d=1 if rev else 0, nb=nb),
        grid=(b // nb, nc_tot),
        in_specs=in_specs,
        out_specs=pl.BlockSpec((nb, CHUNK, ML_W), lambda i, s: (i, cidx(s), 0)),
        out_shape=jax.ShapeDtypeStruct((b, t, ML_W), F32),
        scratch_shapes=[pltpu.VMEM((nb * ML_H, ML_DH, ML_DH), F32),
                        pltpu.VMEM((nb * ML_H, 1, ML_DH), F32),
                        pltpu.VMEM((nb * ML_H, 1, LANES), F32)],
        compiler_params=_params(2, arbitrary_last=True),
        name="mlstm_bwd" if rev else "mlstm_fwd",
    )(*args, *consts)


def _conv_kernel(x_ref, mod_ref, g_ref, w_ref, dw_ref, db_ref, lnw_ref, lnb_ref, out_ref, pad_ref,
                 *, n_ctx_tiles):
    d = D_MODEL
    mod = mod_ref[0]
    h = _norm_mod(x_ref[0], g_ref[...], mod[:, 0:d], mod[:, d:2 * d])
    u = _mm(h, w_ref[...])
    z = u[:, :CONV_W] * _sigmoid(u[:, CONV_W:])
    dw = dw_ref[...]
    half = CONV_K // 2

    def conv_segments(seg_len):
        stride = seg_len + 2 * CONV_PAD
        zeros = jnp.zeros((CONV_PAD, CONV_W), F32)
        for i in range(TILE // seg_len):
            base = i * stride
            pad_ref[base:base + CONV_PAD, :] = zeros
            pad_ref[base + CONV_PAD:base + CONV_PAD + seg_len, :] = z[i * seg_len:(i + 1) * seg_len]
            pad_ref[base + CONV_PAD + seg_len:base + stride, :] = zeros
        for blk in range(TILE // GRID_W):
            row0 = blk * GRID_W
            seg, off = divmod(row0, seg_len)
            base = seg * stride + CONV_PAD - half + off
            acc = jnp.zeros((GRID_W, CONV_W), F32)
            for res in range(SUBLANES):
                n_q = (CONV_K - 1 - res) // SUBLANES + 1
                win = pad_ref[base + res:base + res + SUBLANES * (n_q - 1) + GRID_W, :]
                for q in range(n_q):
                    j = SUBLANES * q + res
                    acc = acc + dw[j:j + 1, :] * win[SUBLANES * q:SUBLANES * q + GRID_W]
            y = _layernorm_lanes(acc + db_ref[...], LN_EPS) * lnw_ref[...] + lnb_ref[...]
            out_ref[0, row0:row0 + GRID_W, :] = y * _sigmoid(y)

    is_ctx = pl.program_id(1) < n_ctx_tiles

    @pl.when(is_ctx)
    def _():
        conv_segments(TILE)

    @pl.when(jnp.logical_not(is_ctx))
    def _():
        conv_segments(GRID_W)


def _conv_branch(xa, mod, g, prm, ctx_row, n_ctx_tiles):
    b, t, d = xa.shape
    consts = [g, prm["w_cv"], prm["dw"], prm["db"], prm["lnw"], prm["lnb"]]
    pad_rows = (TILE // GRID_W) * (GRID_W + 2 * CONV_PAD)
    return pl.pallas_call(
        functools.partial(_conv_kernel, n_ctx_tiles=n_ctx_tiles),
        grid=(b, t // TILE),
        in_specs=[_tile_spec(d), _mod_spec(ctx_row, n_ctx_tiles)] + [_const_spec(c.shape) for c in consts],
        out_specs=_tile_spec(CONV_W),
        out_shape=jax.ShapeDtypeStruct((b, t, CONV_W), F32),
        scratch_shapes=[pltpu.VMEM((pad_rows, CONV_W), F32)],
        compiler_params=_params(2),
        name="conv_branch",
    )(xa, mod, *consts)


def _merge_kernel(x_ref, mod_ref, g_ref, ya_ref, yb_ref, yc_ref, wg_ref, pa_ref, pb_ref, pc_ref, wo_ref,
                  out_ref):
    d = D_MODEL
    mod = mod_ref[0]
    x = x_ref[0]
    h = _norm_mod(x, g_ref[...], mod[:, 0:d], mod[:, d:2 * d])
    ug = _sigmoid(_mm(h, wg_ref[...]))
    m = (ug[:, 0:d] * _mm(ya_ref[0], pa_ref[...])
         + ug[:, d:2 * d] * _mm(yb_ref[0], pb_ref[...])
         + ug[:, 2 * d:3 * d] * _mm(yc_ref[0], pc_ref[...]))
    out_ref[0] = x + mod[:, 2 * d:3 * d] * _mm(m, wo_ref[...])


def _merge(xa, mod, g, ya, yb, yc, prm, ctx_row, n_ctx_tiles):
    b, t, d = xa.shape
    consts = [prm["w_gate"], prm["p_a"], prm["p_b"], prm["p_c"], prm["w_out"]]
    return pl.pallas_call(
        _merge_kernel,
        grid=(b, t // TILE),
        in_specs=[_tile_spec(d), _mod_spec(ctx_row, n_ctx_tiles), _const_spec(g.shape),
                  _tile_spec(RW_W), _tile_spec(CONV_W), _tile_spec(ML_W)]
        + [_const_spec(c.shape) for c in consts],
        out_specs=_tile_spec(d),
        out_shape=jax.ShapeDtypeStruct((b, t, d), F32),
        compiler_params=_params(2),
        name="merge",
    )(xa, mod, g, ya, yb, yc, *consts)


FF_CHUNK = 1024


def _mlp_kernel(x_ref, mod_ref, g_ref, w1_ref, w2_ref, out_ref):
    d = D_MODEL
    mod = mod_ref[0]
    x = x_ref[0]
    h = _norm_mod(x, g_ref[...], mod[:, 3 * d:4 * d], mod[:, 4 * d:5 * d]).astype(BF16)
    acc = jnp.zeros((TILE, d), F32)
    for i in range(D_FF // FF_CHUNK):
        hid = jnp.maximum(_mm(h, w1_ref[:, i * FF_CHUNK:(i + 1) * FF_CHUNK]), 0.0)
        acc = acc + _mm(hid * hid, w2_ref[i * FF_CHUNK:(i + 1) * FF_CHUNK, :])
    out_ref[0] = x + mod[:, 5 * d:6 * d] * acc


def _mlp(xa, mod, g, w1, w2, ctx_row, n_ctx_tiles):
    b, t, d = xa.shape
    return pl.pallas_call(
        _mlp_kernel,
        grid=(b, t // TILE),
        in_specs=[_tile_spec(d), _mod_spec(ctx_row, n_ctx_tiles), _const_spec(g.shape),
                  _const_spec(w1.shape), _const_spec(w2.shape)],
        out_specs=_tile_spec(d),
        out_shape=jax.ShapeDtypeStruct((b, t, d), F32),
        compiler_params=_params(2),
        name="mlp",
    )(xa, mod, g, w1, w2)


def _final_kernel(x_ref, g_ref, out_ref):
    x = x_ref[0]
    out_ref[0] = x * lax.rsqrt(jnp.mean(x * x, axis=-1, keepdims=True) + NORM_EPS) * g_ref[...]


def _final_norm(xa, g, n_ctx_tiles):
    b, t, d = xa.shape
    n_lat = t // TILE - n_ctx_tiles
    return pl.pallas_call(
        _final_kernel,
        grid=(b, n_lat),
        in_specs=[pl.BlockSpec((1, TILE, d), lambda i, j: (i, j + n_ctx_tiles, 0)), _const_spec(g.shape)],
        out_specs=pl.BlockSpec((1, TILE, d), lambda i, j: (i, j, 0)),
        out_shape=jax.ShapeDtypeStruct((b, n_lat * TILE, d), F32),
        compiler_params=_params(2),
        name="final_norm",
    )(xa, g)


def _pad_rows(m, rows, offset):
    out = jnp.zeros((rows, m.shape[1]), m.dtype)
    return out.at[offset:offset + m.shape[0]].set(m)


def kernel(x, c, ctx, c_ctx, w_ada, b_ada, g_norm1, g_norm2, w_in, mu_prev, mu_next,
           rw_w0, rw_w2, rw_a0, rw_a2, rw_kk, rw_ka, rw_rk, rw_g2, rw_lnw, rw_lnb,
           cv_dw, cv_db, cv_lnw, cv_lnb, ml_ib, ml_fb, ml_nw,
           p_a, p_b, p_c, w_out, w_mlp1, w_mlp2, g_final):
    batch, seq, d = x.shape
    ctx_len = ctx.shape[1]
    depth = w_ada.shape[0]
    assert d == D_MODEL and ctx_len == TILE and seq % TILE == 0
    n_ctx_tiles = ctx_len // TILE
    nc_ctx = ctx_len // CHUNK

    ctx_row = batch
    rows = -(-(batch + 1) // SUBLANES) * SUBLANES
    c_all = jnp.zeros((rows, d), F32).at[:batch].set(c).at[ctx_row].set(c_ctx)
    mods = _ada_mods(c_all, w_ada, b_ada)

    lane = jnp.arange(GW)
    mask_bd = (lane[:, None] // RW_N == lane[None, :] // RW_N).astype(BF16)
    lane = jnp.arange(RW_W)
    ones_bd = (lane[:, None] // RW_N == lane[None, :] // RW_N).astype(BF16)

    xa = jnp.concatenate([ctx, x], axis=1)
    for l in range(depth):
        mod = mods[l].reshape(rows, 1, N_MOD * d)
        g1 = g_norm1[l].reshape(1, d)
        g2 = g_norm2[l].reshape(1, d)
        wl = w_in[l]
        gate_w = _pad_rows(wl[:, OFF_ML + 4 * ML_W:OFF_GATE].T, MG_COLS, 0).T
        w_rec = jnp.concatenate([wl[:, :RW_COLS], wl[:, OFF_ML:OFF_ML + 4 * ML_W], gate_w],
                                axis=1).astype(BF16)
        u_rw, u_ml, u_mg = _proj_rec(xa, mod, g1, w_rec, ctx_row, n_ctx_tiles)

        rw = {
            "mu_p": mu_prev[l].reshape(1, RW_COLS), "mu_n": mu_next[l].reshape(1, RW_COLS),
            "w0": rw_w0[l].reshape(2, 1, RW_W), "a0": rw_a0[l].reshape(2, 1, RW_W),
            "w2": jnp.stack([_pad_rows(rw_w2[l, i], 2 * LORA_DECAY, LORA_DECAY * i)
                             for i in range(2)]).astype(BF16),
            "a2": jnp.stack([_pad_rows(rw_a2[l, i], 2 * LORA_ICLR, LORA_ICLR * i)
                             for i in range(2)]).astype(BF16),
            "kk": rw_kk[l].reshape(1, RW_W), "ka": rw_ka[l].reshape(1, RW_W),
            "rk": rw_rk[l].reshape(1, RW_W), "g2": rw_g2[l].astype(BF16),
            "lnw": rw_lnw[l].reshape(1, RW_W), "lnb": rw_lnb[l].reshape(1, RW_W),
            "ones_bd": ones_bd, "mask_bd": mask_bd,
        }
        ya = _rwkv_mixer(u_rw, rw, ctx_len)

        gate_bias = jnp.zeros((1, MG_COLS), F32).at[0, :4 * ML_H].set(
            jnp.stack([ml_ib[l], ml_fb[l]], axis=1).reshape(-1))
        ml = {"gate_bias": gate_bias, "nw": ml_nw[l].reshape(1, ML_W)}
        h_b = _mlstm_pass(u_ml, u_mg, None, ml, True, nc_ctx)
        yc = _mlstm_pass(u_ml, u_mg, h_b, ml, False, nc_ctx)

        cv = {"w_cv": wl[:, OFF_CV:OFF_ML].astype(BF16), "dw": cv_dw[l],
              "db": cv_db[l].reshape(1, CONV_W), "lnw": cv_lnw[l].reshape(1, CONV_W),
              "lnb": cv_lnb[l].reshape(1, CONV_W)}
        yb = _conv_branch(xa, mod, g1, cv, ctx_row, n_ctx_tiles)

        mg = {"w_gate": wl[:, OFF_GATE:].astype(BF16), "p_a": p_a[l].astype(BF16),
              "p_b": p_b[l].astype(BF16), "p_c": p_c[l].astype(BF16), "w_out": w_out[l].astype(BF16)}
        xa = _merge(xa, mod, g1, ya, yb, yc, mg, ctx_row, n_ctx_tiles)
        xa = _mlp(xa, mod, g2, w_mlp1[l].astype(BF16), w_mlp2[l].astype(BF16), ctx_row, n_ctx_tiles)
    return _final_norm(xa, g_final.reshape(1, d), n_ctx_tiles)
```

```python
import functools

import jax
import jax.numpy as jnp
from jax import lax
from jax.experimental import pallas as pl
from jax.experimental.pallas import tpu as pltpu

F32 = jnp.float32
BF16 = jnp.bfloat16

D_MODEL = 1024
GRID_W = 64
N_MOD = 6
NORM_EPS = 1e-6
LN_EPS = 1e-5
RW_N = 64
RW_W = D_MODEL // 2
RW_H = RW_W // RW_N
LORA_DECAY = 64
LORA_ICLR = 64
LORA_GATE = 128
RW_GN_EPS = 64e-5
CONV_W = D_MODEL // 2
CONV_K = 31
ML_H = 4
ML_W = D_MODEL // 2
ML_DH = ML_W // ML_H
N_BRANCH = 3
D_FF = 4 * D_MODEL
RW_COLS = 3 * RW_W + 2 * LORA_DECAY + 2 * LORA_ICLR + LORA_GATE
CV_COLS = 2 * CONV_W
ML_COLS = 4 * ML_W + 4 * ML_H
GATE_COLS = N_BRANCH * D_MODEL
OFF_CV = RW_COLS
OFF_ML = OFF_CV + CV_COLS
OFF_GATE = OFF_ML + ML_COLS

LANES = 128
SUBLANES = 8
TILE = 256
CHUNK = 64
HGROUP = 4
GW = HGROUP * RW_N
N_GROUPS = RW_H // HGROUP
MG_COLS = LANES
CONV_PAD = 16
VMEM_LIMIT = 56 * 1024 * 1024


def _mm(a, b):
    return jnp.dot(a.astype(BF16), b.astype(BF16), preferred_element_type=F32)


def _mm_nt(a, b):
    return lax.dot_general(a.astype(BF16), b.astype(BF16), (((1,), (1,)), ((), ())),
                           preferred_element_type=F32)


def _mm_tn(a, b):
    return lax.dot_general(a.astype(BF16), b.astype(BF16), (((0,), (0,)), ((), ())),
                           preferred_element_type=F32)


def _split3(x):
    hi = x.astype(BF16)
    r1 = x - hi.astype(F32)
    mid = r1.astype(BF16)
    lo = (r1 - mid.astype(F32)).astype(BF16)
    return hi, mid, lo


def _mm_exact_lhs(t, x):
    tb = t.astype(BF16)
    hi, mid, lo = _split3(x)
    dot = functools.partial(jnp.dot, preferred_element_type=F32)
    return dot(tb, hi) + dot(tb, mid) + dot(tb, lo)


def _mm_exact_rhs_nt(x, t):
    tb = t.astype(BF16)
    hi, mid, lo = _split3(x)
    dn = (((1,), (1,)), ((), ()))
    dot = functools.partial(lax.dot_general, dimension_numbers=dn, preferred_element_type=F32)
    return dot(hi, tb) + dot(mid, tb) + dot(lo, tb)


def _segsum(x, ones_bd):
    hi = x.astype(BF16)
    lo = (x - hi.astype(F32)).astype(BF16)
    dot = functools.partial(jnp.dot, preferred_element_type=F32)
    gw = ones_bd.shape[0]
    return jnp.concatenate(
        [dot(hi[:, i:i + gw], ones_bd) + dot(lo[:, i:i + gw], ones_bd) for i in range(0, x.shape[1], gw)],
        axis=1)


def _sigmoid(x):
    return 1.0 / (1.0 + jnp.exp(-x))


def _softplus(x):
    return jnp.maximum(x, 0.0) + jnp.log(1.0 + jnp.exp(-jnp.abs(x)))


def _norm_mod(x, g, shift, scale):
    y = x * lax.rsqrt(jnp.mean(x * x, axis=-1, keepdims=True) + NORM_EPS)
    return (y * g) * (1.0 + scale) + shift


def _layernorm_lanes(x, eps):
    xc = x - jnp.mean(x, axis=-1, keepdims=True)
    return xc * lax.rsqrt(jnp.mean(xc * xc, axis=-1, keepdims=True) + eps)


def _const_spec(shape):
    nd = len(shape)
    return pl.BlockSpec(shape, lambda *_: (0,) * nd)


def _params(n_axes, arbitrary_last=False):
    sem = ["parallel"] * n_axes
    if arbitrary_last:
        sem[-1] = "arbitrary"
    return pltpu.CompilerParams(dimension_semantics=tuple(sem), vmem_limit_bytes=VMEM_LIMIT)


def _ada_kernel(c_ref, w_ref, b_ref, o_ref):
    c = c_ref[...]
    o_ref[0] = _mm(c * _sigmoid(c), w_ref[0]) + b_ref[0]


def _ada_mods(c_all, w_ada, b_ada):
    depth, d, nd = w_ada.shape
    rows = c_all.shape[0]
    return pl.pallas_call(
        _ada_kernel,
        grid=(depth, nd // d),
        in_specs=[pl.BlockSpec((rows, d), lambda l, j: (0, 0)),
                  pl.BlockSpec((1, d, d), lambda l, j: (l, 0, j)),
                  pl.BlockSpec((1, 1, d), lambda l, j: (l, 0, j))],
        out_specs=pl.BlockSpec((1, rows, d), lambda l, j: (l, 0, j)),
        out_shape=jax.ShapeDtypeStruct((depth, rows, nd), F32),
        compiler_params=_params(2),
        name="ada_mods",
    )(c_all, w_ada, b_ada.reshape(depth, 1, nd))


def _tile_spec(width):
    return pl.BlockSpec((1, TILE, width), lambda b, j: (b, j, 0))


def _mod_spec(ctx_row, n_ctx_tiles):
    return pl.BlockSpec((1, 1, N_MOD * D_MODEL),
                        lambda b, j: (jnp.where(j < n_ctx_tiles, ctx_row, b), 0, 0))


def _proj_kernel(x_ref, mod_ref, g_ref, w_ref, urw_ref, uml_ref, umg_ref):
    d = D_MODEL
    mod = mod_ref[0]
    h = _norm_mod(x_ref[0], g_ref[...], mod[:, 0:d], mod[:, d:2 * d])
    u = _mm(h, w_ref[...])
    urw_ref[0] = u[:, :RW_COLS]
    uml_ref[0] = u[:, RW_COLS:RW_COLS + 4 * ML_W]
    umg_ref[0] = u[:, RW_COLS + 4 * ML_W:]


def _proj_rec(xa, mod, g, w_rec, ctx_row, n_ctx_tiles):
    b, t, d = xa.shape
    ncol = w_rec.shape[1]
    return pl.pallas_call(
        _proj_kernel,
        grid=(b, t // TILE),
        in_specs=[_tile_spec(d), _mod_spec(ctx_row, n_ctx_tiles), _const_spec((1, d)),
                  _const_spec((d, ncol))],
        out_specs=[_tile_spec(RW_COLS), _tile_spec(4 * ML_W), _tile_spec(MG_COLS)],
        out_shape=[jax.ShapeDtypeStruct((b, t, RW_COLS), F32),
                   jax.ShapeDtypeStruct((b, t, 4 * ML_W), F32),
                   jax.ShapeDtypeStruct((b, t, MG_COLS), F32)],
        compiler_params=_params(2),
        name="proj_rec",
    )(xa, mod, g, w_rec)


def _chunk_index(s, rev, nc_ctx, nc_tot):
    if not rev:
        return s
    return jnp.where(s < nc_ctx, nc_ctx - 1 - s, nc_tot - 1 - (s - nc_ctx))


INV_BASE = 8


def _unit_tri_inverse(l_ms, bd, eye, t_id, i_id):
    def same_block(n):
        sh = n.bit_length() - 1
        return lax.shift_right_logical(t_id, sh) == lax.shift_right_logical(i_id, sh)

    diag = same_block(INV_BASE)
    l_d = [jnp.where(diag, l_m, 0.0) for l_m in l_ms]
    a = [eye - m for m in l_d]
    p2 = [_mm(m, bd(m)) for m in l_d]
    t_m = [x + _mm(x, bd(p)) for x, p in zip(a, p2)]
    p4 = [_mm(p, bd(p)) for p in p2]
    t_m = [t + _mm(t, bd(p)) for t, p in zip(t_m, p4)]
    n = INV_BASE
    while n < CHUNK:
        sub = jnp.logical_and(same_block(2 * n), jnp.logical_not(same_block(n)))
        g_m = [_mm(jnp.where(sub, l_m, 0.0), bd(t)) for l_m, t in zip(l_ms, t_m)]
        t_m = [t - _mm(t, bd(g)) for t, g in zip(t_m, g_m)]
        n *= 2
    return t_m


RW_TILE = 2 * CHUNK
RW_SCAN_BATCH = 8


def _shifted_tile(u_ref, up_ref, un_ref, buf_ref, mu_p, mu_n, is_first, is_last, rows):
    u = u_ref[0]
    hs = SUBLANES
    buf_ref[hs:hs + rows, :] = u
    buf_ref[hs - 1:hs, :] = jnp.where(is_first, 0.0, up_ref[0, hs - 1:hs, :])
    buf_ref[hs + rows:hs + rows + 1, :] = jnp.where(is_last, 0.0, un_ref[0, 0:1, :])
    prev = buf_ref[hs - 1:hs - 1 + rows, :]
    nxt = buf_ref[hs + 1:hs + 1 + rows, :]
    return u + mu_p * (prev - u) + mu_n * (nxt - u)


def _halo_specs(rows, n8):
    per8 = rows // SUBLANES
    return [pl.BlockSpec((1, rows, RW_COLS), lambda i, j: (i, j, 0)),
            pl.BlockSpec((1, SUBLANES, RW_COLS), lambda i, j: (i, jnp.maximum(j * per8 - 1, 0), 0)),
            pl.BlockSpec((1, SUBLANES, RW_COLS), lambda i, j: (i, jnp.minimum((j + 1) * per8, n8 - 1), 0))]


def _rwkv_pre_kernel(u_ref, up_ref, un_ref, mup_ref, mun_ref, w0_ref, w2_ref, a0_ref, a2_ref, kk_ref, ka_ref,
                     ones_ref, mbd_ref, y0_ref, rt_ref, gm_ref, dm_ref, wt_ref, buf_ref, *, ctx_tiles, n_tiles):
    j = pl.program_id(1)
    is_first = jnp.logical_or(j == 0, j == ctx_tiles)
    is_last = jnp.logical_or(j == ctx_tiles - 1, j == n_tiles - 1)
    ush = _shifted_tile(u_ref, up_ref, un_ref, buf_ref, mup_ref[...], mun_ref[...], is_first, is_last, RW_TILE)

    w = RW_W
    r = ush[:, 0:w]
    k = ush[:, w:2 * w]
    v = ush[:, 2 * w:3 * w]
    wd = jnp.tanh(ush[:, 3 * w:3 * w + 2 * LORA_DECAY])
    ad = ush[:, 3 * w + 2 * LORA_DECAY:3 * w + 2 * LORA_DECAY + 2 * LORA_ICLR]
    kkr = k * kk_ref[...]
    kk = kkr / jnp.maximum(jnp.sqrt(_segsum(kkr * kkr, ones_ref[...])), 1e-12)

    row = lax.broadcasted_iota(jnp.int32, (CHUNK, CHUNK), 0)
    col = lax.broadcasted_iota(jnp.int32, (CHUNK, CHUNK), 1)
    t_id = lax.broadcasted_iota(jnp.int32, (CHUNK, GW), 0)
    i_id = jnp.bitwise_and(lax.broadcasted_iota(jnp.int32, (CHUNK, GW), 1), RW_N - 1)
    eye = jnp.where(i_id == t_id, 1.0, 0.0)
    mbd = mbd_ref[...]
    mbd_f2 = jnp.concatenate([mbd, mbd], axis=1).astype(F32)
    zeros = jnp.zeros((CHUNK, GW), F32)

    def bd(m):
        mb = m.astype(BF16)
        return jnp.concatenate([mb] * HGROUP, axis=0) * mbd

    logw_d, k_d_d, kka_d = [], [], []
    for d in range(2):
        w_log = -_softplus(-(w0_ref[d] + _mm(wd, w2_ref[d]))) - 0.5
        logw_d.append(-jnp.exp(w_log))
        a = _sigmoid(a0_ref[d] + _mm(ad, a2_ref[d]))
        k_d_d.append(k * (1.0 + (a - 1.0) * ka_ref[...]))
        kka_d.append(kk * a)
    units = [(d, ci) for d in range(2) for ci in range(RW_TILE // CHUNK)]
    rows_of = lambda ci: slice(ci * CHUNK, (ci + 1) * CHUNK)
    tris = [jnp.where((col >= row) if d == 1 else (col <= row), 1.0, 0.0) for d in range(2)]
    cums = [_mm_exact_lhs(tris[d], logw_d[d][rows_of(ci)]) for d, ci in units]

    chains = []
    for (d, ci), c in zip(units, cums):
        rev = d == 1
        rows = rows_of(ci)
        logw = logw_d[d][rows]
        k_d = k_d_d[d][rows]
        kka = kka_d[d][rows]
        last = 0 if rev else CHUNK - 1
        c_tot = c[last:last + 1, :]
        e_neg = jnp.exp(-c)
        e_rem = jnp.exp(c_tot - c)
        alpha = kk[rows] * jnp.exp(c - logw)
        beta = kka * e_neg
        kappa = k_d * e_neg
        rho = r[rows] * jnp.exp(c)
        kap_hat = k_d * e_rem
        bet_hat = kka * e_rem
        wt_ref[d, 0, ci] = jnp.exp(c_tot)
        strict = (i_id > t_id) if rev else (i_id < t_id)
        nonstrict = (i_id >= t_id) if rev else (i_id <= t_id)
        for g in range(N_GROUPS):
            sl = slice(GW * g, GW * (g + 1))
            chains.append(dict(alpha=alpha[:, sl], beta=beta[:, sl], kappa=kappa[:, sl], rho=rho[:, sl],
                               kap_hat=kap_hat[:, sl], bet_hat=bet_hat[:, sl], v=v[rows, sl],
                               strict=strict, nonstrict=nonstrict))

    ar = [jnp.concatenate([ch["alpha"], ch["rho"]], axis=0) for ch in chains]
    ab = [_mm_nt(x, bd(ch["beta"])) for x, ch in zip(ar, chains)]
    ak = [_mm_nt(x, bd(ch["kappa"])) for x, ch in zip(ar, chains)]
    l_m = [jnp.where(ch["strict"], x[:CHUNK], 0.0) for x, ch in zip(ab, chains)]
    p_b = [jnp.where(ch["nonstrict"], x[CHUNK:], 0.0) for x, ch in zip(ab, chains)]
    akm = [jnp.concatenate([jnp.where(ch["strict"], x[:CHUNK], 0.0),
                            jnp.where(ch["nonstrict"], x[CHUNK:], 0.0)], axis=0) for x, ch in zip(ak, chains)]
    akv = [_mm(x, bd(ch["v"])) for x, ch in zip(akm, chains)]
    t_m = _unit_tri_inverse(l_m, bd, eye, t_id, i_id)
    tu = [_mm(t, jnp.concatenate([bd(x[:CHUNK]), bd(ch["alpha"])], axis=1)) for t, x, ch in zip(t_m, akv, chains)]
    pb = [_mm(p, jnp.concatenate([bd(x[:, :GW]), bd(x[:, GW:])], axis=1)) for p, x in zip(p_b, tu)]
    y0 = [x[CHUNK:] - p[:, :GW] for x, p in zip(akv, pb)]
    rt = [ch["rho"] - p[:, GW:] for ch, p in zip(chains, pb)]
    full = [_mm_tn(jnp.concatenate([ch["v"], x[:, :GW], x[:, GW:]], axis=0),
                   jnp.concatenate(
                       [jnp.concatenate([ch["kap_hat"], -ch["bet_hat"], zeros], axis=0),
                        jnp.concatenate([zeros, zeros, ch["bet_hat"]], axis=0)], axis=1)) * mbd_f2
            for ch, x in zip(chains, tu)]
    blk = [sum(x[RW_N * h:RW_N * (h + 1)] for h in range(1, HGROUP)) + x[0:RW_N] for x in full]

    for ui, (d, ci) in enumerate(units):
        rows = rows_of(ci)
        sel = range(ui * N_GROUPS, (ui + 1) * N_GROUPS)
        y0_ref[d, 0, rows, :] = jnp.concatenate([y0[i] for i in sel], axis=1)
        rt_ref[d, 0, rows, :] = jnp.concatenate([rt[i] for i in sel], axis=1).astype(BF16)
        gm_ref[d, 0, rows, :] = jnp.concatenate([blk[i][:, GW:] for i in sel], axis=1).astype(BF16)
        dm_ref[d, 0, rows, :] = jnp.concatenate([blk[i][:, :GW] for i in sel], axis=1)


def _rwkv_pre(u_rw, prm, ctx_len):
    b, t, _ = u_rw.shape
    n_tiles = t // RW_TILE
    nc = t // CHUNK
    cpt = RW_TILE // CHUNK
    consts = [prm["mu_p"], prm["mu_n"], prm["w0"], prm["w2"], prm["a0"], prm["a2"], prm["kk"], prm["ka"],
              prm["ones_bd"], prm["mask_bd"]]
    tile_out = pl.BlockSpec((2, 1, RW_TILE, RW_W), lambda i, j: (0, i, j, 0))
    return pl.pallas_call(
        functools.partial(_rwkv_pre_kernel, ctx_tiles=ctx_len // RW_TILE, n_tiles=n_tiles),
        grid=(b, n_tiles),
        in_specs=_halo_specs(RW_TILE, t // SUBLANES) + [_const_spec(c.shape) for c in consts],
        out_specs=[tile_out, tile_out, tile_out, tile_out,
                   pl.BlockSpec((2, 1, cpt, 1, RW_W), lambda i, j: (0, i, j, 0, 0))],
        out_shape=[jax.ShapeDtypeStruct((2, b, t, RW_W), F32),
                   jax.ShapeDtypeStruct((2, b, t, RW_W), BF16),
                   jax.ShapeDtypeStruct((2, b, t, RW_W), BF16),
                   jax.ShapeDtypeStruct((2, b, t, RW_W), F32),
                   jax.ShapeDtypeStruct((2, b, nc, 1, RW_W), F32)],
        scratch_shapes=[pltpu.VMEM((RW_TILE + 2 * SUBLANES, RW_COLS), F32)],
        compiler_params=_params(2),
        name="rwkv_pre",
    )(u_rw, u_rw, u_rw, *consts)


def _rwkv_scan_kernel(y0_ref, rt_ref, gm_ref, dm_ref, wt_ref, mbd_ref, y_ref, st_ref, *, nb):
    @pl.when(pl.program_id(2) == 0)
    def _():
        st_ref[...] = jnp.zeros_like(st_ref)

    mbd = mbd_ref[...]

    def bd(mb):
        return jnp.concatenate([mb] * HGROUP, axis=0) * mbd

    for bi in range(nb):
        ys = []
        for g in range(N_GROUPS):
            sl = slice(GW * g, GW * (g + 1))
            s0 = st_ref[bi, g]
            s0b = s0.astype(BF16)
            ys.append(y0_ref[0, bi, :, sl] + lax.dot_general(
                rt_ref[0, bi, :, sl], bd(s0b), (((1,), (1,)), ((), ())), preferred_element_type=F32))
            st_ref[bi, g] = (s0 * wt_ref[0, bi, 0, :, sl]
                             - jnp.dot(s0b, bd(gm_ref[0, bi, :, sl]), preferred_element_type=F32)
                             + dm_ref[0, bi, :, sl])
        y_ref[0, bi] = jnp.concatenate(ys, axis=1)


def _rwkv_scan(y0, rt, gm, dm, wt, mask_bd, nc_ctx):
    _, b, t, _ = y0.shape
    nc_tot = t // CHUNK
    nb = RW_SCAN_BATCH if b % RW_SCAN_BATCH == 0 else 1

    def cidx(d, s):
        rev = jnp.where(s < nc_ctx, nc_ctx - 1 - s, nc_tot - 1 - (s - nc_ctx))
        return jnp.where(d == 1, rev, s)

    chunk = pl.BlockSpec((1, nb, CHUNK, RW_W), lambda d, i, s: (d, i, cidx(d, s), 0))
    return pl.pallas_call(
        functools.partial(_rwkv_scan_kernel, nb=nb),
        grid=(2, b // nb, nc_tot),
        in_specs=[chunk, chunk, chunk, chunk,
                  pl.BlockSpec((1, nb, 1, 1, RW_W), lambda d, i, s: (d, i, cidx(d, s), 0, 0)),
                  _const_spec(mask_bd.shape)],
        out_specs=chunk,
        out_shape=jax.ShapeDtypeStruct(y0.shape, F32),
        scratch_shapes=[pltpu.VMEM((nb, N_GROUPS, RW_N, GW), F32)],
        compiler_params=_params(3, arbitrary_last=True),
        name="rwkv_scan",
    )(y0, rt, gm, dm, wt, mask_bd)


def _rwkv_out_kernel(u_ref, up_ref, un_ref, y_ref, mup_ref, mun_ref, rk_ref, g2_ref, lnw_ref, lnb_ref, ones_ref,
                     out_ref, buf_ref, *, ctx_tiles, n_tiles):
    j = pl.program_id(1)
    is_first = jnp.logical_or(j == 0, j == ctx_tiles)
    is_last = jnp.logical_or(j == ctx_tiles - 1, j == n_tiles - 1)
    ush = _shifted_tile(u_ref, up_ref, un_ref, buf_ref, mup_ref[...], mun_ref[...], is_first, is_last, TILE)
    w = RW_W
    r = ush[:, 0:w]
    k = ush[:, w:2 * w]
    v = ush[:, 2 * w:3 * w]
    gd = ush[:, 3 * w + 2 * LORA_DECAY + 2 * LORA_ICLR:]
    ones_bd = ones_ref[...]
    y = y_ref[0, 0] + y_ref[1, 0]
    mean = _segsum(y, ones_bd) * (1.0 / RW_N)
    yc = y - mean
    var = _segsum(yc * yc, ones_bd) * (1.0 / RW_N)
    yn = yc * lax.rsqrt(var + RW_GN_EPS) * lnw_ref[...] + lnb_ref[...]
    bonus = _segsum(r * k * rk_ref[...], ones_bd) * v
    out_ref[0] = (yn + bonus) * _mm(_sigmoid(gd), g2_ref[...])


def _rwkv_out(u_rw, y, prm, ctx_len):
    b, t, _ = u_rw.shape
    n_tiles = t // TILE
    consts = [prm["mu_p"], prm["mu_n"], prm["rk"], prm["g2"], prm["lnw"], prm["lnb"], prm["ones_bd"]]
    return pl.pallas_call(
        functools.partial(_rwkv_out_kernel, ctx_tiles=ctx_len // TILE, n_tiles=n_tiles),
        grid=(b, n_tiles),
        in_specs=_halo_specs(TILE, t // SUBLANES)
        + [pl.BlockSpec((2, 1, TILE, RW_W), lambda i, j: (0, i, j, 0))]
        + [_const_spec(c.shape) for c in consts],
        out_specs=_tile_spec(RW_W),
        out_shape=jax.ShapeDtypeStruct((b, t, RW_W), F32),
        scratch_shapes=[pltpu.VMEM((TILE + 2 * SUBLANES, RW_COLS), F32)],
        compiler_params=_params(2),
        name="rwkv_out",
    )(u_rw, u_rw, u_rw, y, *consts)


def _rwkv_mixer(u_rw, prm, ctx_len):
    y0, rt, gm, dm, wt = _rwkv_pre(u_rw, prm, ctx_len)
    y = _rwkv_scan(y0, rt, gm, dm, wt, prm["mask_bd"], ctx_len // CHUNK)
    return _rwkv_out(u_rw, y, prm, ctx_len)


ML_BATCH = 4
ML_CHUNK = 128


def _mlstm_kernel(u_ref, g_ref, bias_ref, out_ref, c_ref, n_ref, m_ref, *, rev, d, nb):
    s = pl.program_id(1)

    @pl.when(s == 0)
    def _():
        c_ref[...] = jnp.zeros_like(c_ref)
        n_ref[...] = jnp.zeros_like(n_ref)
        m_ref[...] = jnp.zeros_like(m_ref)

    lane = lax.broadcasted_iota(jnp.int32, (ML_CHUNK, MG_COLS), 1)
    is_forget = jnp.bitwise_and(lane, ML_H) != 0
    row = lax.broadcasted_iota(jnp.int32, (ML_CHUNK, ML_CHUNK), 0)
    col = lax.broadcasted_iota(jnp.int32, (ML_CHUNK, ML_CHUNK), 1)
    seen = (col >= row) if rev else (col <= row)
    tri = jnp.where(seen, 1.0, 0.0)
    last = 0 if rev else ML_CHUNK - 1

    us = [u_ref[bi] for bi in range(nb)]
    gps = [g_ref[bi] + bias_ref[...] for bi in range(nb)]
    gates = [jnp.where(is_forget, -_softplus(-gp), gp) for gp in gps]
    gates_t = [g.T for g in gates]
    cum_col = [_mm_exact_lhs(tri, g) for g in gates]
    cum_row = [_mm_exact_rhs_nt(g, tri) for g in gates_t]

    chains = [(bi, h) for bi in range(nb) for h in range(ML_H)]
    jf = lambda h: 2 * ML_H * d + ML_H + h
    ji = lambda h: 2 * ML_H * d + h
    q = [us[bi][:, ML_DH * h:ML_DH * (h + 1)] for bi, h in chains]
    k = [us[bi][:, ML_W + ML_DH * h:ML_W + ML_DH * (h + 1)] * (ML_DH ** -0.5) for bi, h in chains]
    v = [us[bi][:, 2 * ML_W + ML_DH * h:2 * ML_W + ML_DH * (h + 1)] for bi, h in chains]
    b_col = [cum_col[bi][:, jf(h):jf(h) + 1] for bi, h in chains]
    b_row = [cum_row[bi][jf(h):jf(h) + 1, :] for bi, h in chains]
    i_col = [gates[bi][:, ji(h):ji(h) + 1] for bi, h in chains]
    i_row = [gates_t[bi][ji(h):ji(h) + 1, :] for bi, h in chains]
    c_st = [c_ref[bi * ML_H + h] for bi, h in chains]
    n_st = [n_ref[bi * ML_H + h] for bi, h in chains]
    m_st = [m_ref[bi * ML_H + h][:, 0:1] for bi, h in chains]
    n_ch = range(len(chains))

    qk = [_mm_nt(q[i], k[i]) for i in n_ch]
    qc = [_mm_nt(q[i], c_st[i]) for i in n_ch]
    dlog = [jnp.where(seen, b_col[i] - b_row[i] + i_row[i], -jnp.inf) for i in n_ch]
    inter = [b_col[i] + m_st[i] for i in n_ch]
    m_t = [jnp.maximum(inter[i], jnp.max(dlog[i], axis=1, keepdims=True)) for i in n_ch]
    s_mat = [qk[i] * jnp.exp(dlog[i] - m_t[i]) for i in n_ch]
    sv = [_mm(s_mat[i], v[i]) for i in n_ch]
    b_last = [b_col[i][last:last + 1, :] for i in n_ch]
    wlog = [b_last[i] - b_col[i] + i_col[i] for i in n_ch]
    m_new = [jnp.maximum(b_last[i] + m_st[i], jnp.max(wlog[i], axis=0, keepdims=True)) for i in n_ch]
    wk = [jnp.exp(wlog[i] - m_new[i]) * k[i] for i in n_ch]
    vwk = [_mm_tn(v[i], wk[i]) for i in n_ch]

    hs = []
    for i, (bi, h) in enumerate(chains):
        w_inter = jnp.exp(inter[i] - m_t[i])
        num = w_inter * qc[i] + sv[i]
        den = (w_inter * jnp.sum(q[i] * n_st[i], axis=1, keepdims=True)
               + jnp.sum(s_mat[i], axis=1, keepdims=True))
        hs.append(num / jnp.maximum(jnp.abs(den), jnp.exp(-m_t[i])))
        carry_w = jnp.exp(b_last[i] + m_st[i] - m_new[i])
        st = bi * ML_H + h
        c_ref[st] = carry_w * c_st[i] + vwk[i]
        n_ref[st] = carry_w * n_st[i] + jnp.sum(wk[i], axis=0, keepdims=True)
        m_ref[st] = jnp.broadcast_to(m_new[i], (1, LANES))

    for bi in range(nb):
        out_ref[bi] = jnp.concatenate(hs[bi * ML_H:(bi + 1) * ML_H], axis=1)


def _mlstm_pass(u_ml, u_mg, gate_bias, rev, ctx_len):
    b, t, _ = u_ml.shape
    nc_tot = t // ML_CHUNK
    nb = ML_BATCH if b % ML_BATCH == 0 else 1
    cidx = functools.partial(_chunk_index, rev=rev, nc_ctx=ctx_len // ML_CHUNK, nc_tot=nc_tot)
    in_specs = [pl.BlockSpec((nb, ML_CHUNK, 3 * ML_W), lambda i, s: (i, cidx(s), 0)),
                pl.BlockSpec((nb, ML_CHUNK, MG_COLS), lambda i, s: (i, cidx(s), 0)),
                _const_spec(gate_bias.shape)]
    return pl.pallas_call(
        functools.partial(_mlstm_kernel, rev=rev, d=1 if rev else 0, nb=nb),
        grid=(b // nb, nc_tot),
        in_specs=in_specs,
        out_specs=pl.BlockSpec((nb, ML_CHUNK, ML_W), lambda i, s: (i, cidx(s), 0)),
        out_shape=jax.ShapeDtypeStruct((b, t, ML_W), F32),
        scratch_shapes=[pltpu.VMEM((nb * ML_H, ML_DH, ML_DH), F32),
                        pltpu.VMEM((nb * ML_H, 1, ML_DH), F32),
                        pltpu.VMEM((nb * ML_H, 1, LANES), F32)],
        compiler_params=_params(2, arbitrary_last=True),
        name="mlstm_bwd" if rev else "mlstm_fwd",
    )(u_ml, u_mg, gate_bias)


def _conv_kernel(x_ref, mod_ref, g_ref, w_ref, dw_ref, db_ref, lnw_ref, lnb_ref, out_ref, pad_ref,
                 *, n_ctx_tiles):
    d = D_MODEL
    mod = mod_ref[0]
    h = _norm_mod(x_ref[0], g_ref[...], mod[:, 0:d], mod[:, d:2 * d])
    u = _mm(h, w_ref[...])
    z = u[:, :CONV_W] * _sigmoid(u[:, CONV_W:])
    dw = dw_ref[...]
    half = CONV_K // 2

    def conv_segments(seg_len):
        stride = seg_len + 2 * CONV_PAD
        zeros = jnp.zeros((CONV_PAD, CONV_W), F32)
        for i in range(TILE // seg_len):
            base = i * stride
            pad_ref[base:base + CONV_PAD, :] = zeros
            pad_ref[base + CONV_PAD:base + CONV_PAD + seg_len, :] = z[i * seg_len:(i + 1) * seg_len]
            pad_ref[base + CONV_PAD + seg_len:base + stride, :] = zeros
        for blk in range(TILE // GRID_W):
            row0 = blk * GRID_W
            seg, off = divmod(row0, seg_len)
            lead = CONV_PAD - half
            win_rows = GRID_W + 2 * CONV_PAD
            win = pad_ref[seg * stride + off:seg * stride + off + win_rows, :]
            acc = jnp.zeros((GRID_W, CONV_W), F32)
            for res in range(SUBLANES):
                rot = win if res == 0 else pltpu.roll(win, win_rows - res, 0)
                for q in range((lead + CONV_K - 1 - res) // SUBLANES + 1):
                    j = SUBLANES * q + res - lead
                    if 0 <= j < CONV_K:
                        acc = acc + dw[j:j + 1, :] * rot[SUBLANES * q:SUBLANES * q + GRID_W]
            y = _layernorm_lanes(acc + db_ref[...], LN_EPS) * lnw_ref[...] + lnb_ref[...]
            out_ref[0, row0:row0 + GRID_W, :] = y * _sigmoid(y)

    is_ctx = pl.program_id(1) < n_ctx_tiles

    @pl.when(is_ctx)
    def _():
        conv_segments(TILE)

    @pl.when(jnp.logical_not(is_ctx))
    def _():
        conv_segments(GRID_W)


def _conv_branch(xa, mod, g, prm, ctx_row, n_ctx_tiles):
    b, t, d = xa.shape
    consts = [g, prm["w_cv"], prm["dw"], prm["db"], prm["lnw"], prm["lnb"]]
    pad_rows = (TILE // GRID_W) * (GRID_W + 2 * CONV_PAD)
    return pl.pallas_call(
        functools.partial(_conv_kernel, n_ctx_tiles=n_ctx_tiles),
        grid=(b, t // TILE),
        in_specs=[_tile_spec(d), _mod_spec(ctx_row, n_ctx_tiles)] + [_const_spec(c.shape) for c in consts],
        out_specs=_tile_spec(CONV_W),
        out_shape=jax.ShapeDtypeStruct((b, t, CONV_W), F32),
        scratch_shapes=[pltpu.VMEM((pad_rows, CONV_W), F32)],
        compiler_params=_params(2),
        name="conv_branch",
    )(xa, mod, *consts)


def _merge_kernel(x_ref, mod_ref, g_ref, ya_ref, yb_ref, hf_ref, hb_ref, o_ref, nw_ref,
                  wg_ref, pa_ref, pb_ref, pc_ref, wo_ref, out_ref):
    d = D_MODEL
    mod = mod_ref[0]
    x = x_ref[0]
    h = _norm_mod(x, g_ref[...], mod[:, 0:d], mod[:, d:2 * d])
    ug = _sigmoid(_mm(h, wg_ref[...]))
    hsum = hf_ref[0] + hb_ref[0]
    hn = jnp.concatenate([_layernorm_lanes(hsum[:, ML_DH * i:ML_DH * (i + 1)], LN_EPS) for i in range(ML_H)],
                         axis=1)
    yc = hn * nw_ref[...] * _sigmoid(o_ref[0])
    m = (ug[:, 0:d] * _mm(ya_ref[0], pa_ref[...])
         + ug[:, d:2 * d] * _mm(yb_ref[0], pb_ref[...])
         + ug[:, 2 * d:3 * d] * _mm(yc, pc_ref[...]))
    out_ref[0] = x + mod[:, 2 * d:3 * d] * _mm(m, wo_ref[...])


def _merge(xa, mod, g, ya, yb, h_f, h_b, u_ml, nw, prm, ctx_row, n_ctx_tiles):
    b, t, d = xa.shape
    consts = [prm["w_gate"], prm["p_a"], prm["p_b"], prm["p_c"], prm["w_out"]]
    o_cols = 3 * ML_W // ML_W
    return pl.pallas_call(
        _merge_kernel,
        grid=(b, t // TILE),
        in_specs=[_tile_spec(d), _mod_spec(ctx_row, n_ctx_tiles), _const_spec(g.shape),
                  _tile_spec(RW_W), _tile_spec(CONV_W), _tile_spec(ML_W), _tile_spec(ML_W),
                  pl.BlockSpec((1, TILE, ML_W), lambda i, j: (i, j, o_cols)), _const_spec(nw.shape)]
        + [_const_spec(c.shape) for c in consts],
        out_specs=_tile_spec(d),
        out_shape=jax.ShapeDtypeStruct((b, t, d), F32),
        compiler_params=_params(2),
        name="merge",
    )(xa, mod, g, ya, yb, h_f, h_b, u_ml, nw, *consts)


FF_CHUNK = 1024


def _mlp_kernel(x_ref, mod_ref, g_ref, w1_ref, w2_ref, out_ref):
    d = D_MODEL
    mod = mod_ref[0]
    x = x_ref[0]
    h = _norm_mod(x, g_ref[...], mod[:, 3 * d:4 * d], mod[:, 4 * d:5 * d]).astype(BF16)
    acc = jnp.zeros((TILE, d), F32)
    for i in range(D_FF // FF_CHUNK):
        hid = jnp.maximum(_mm(h, w1_ref[:, i * FF_CHUNK:(i + 1) * FF_CHUNK]), 0.0)
        acc = acc + _mm(hid * hid, w2_ref[i * FF_CHUNK:(i + 1) * FF_CHUNK, :])
    out_ref[0] = x + mod[:, 5 * d:6 * d] * acc


def _mlp(xa, mod, g, w1, w2, ctx_row, n_ctx_tiles):
    b, t, d = xa.shape
    return pl.pallas_call(
        _mlp_kernel,
        grid=(b, t // TILE),
        in_specs=[_tile_spec(d), _mod_spec(ctx_row, n_ctx_tiles), _const_spec(g.shape),
                  _const_spec(w1.shape), _const_spec(w2.shape)],
        out_specs=_tile_spec(d),
        out_shape=jax.ShapeDtypeStruct((b, t, d), F32),
        compiler_params=_params(2),
        name="mlp",
    )(xa, mod, g, w1, w2)


def _final_kernel(x_ref, g_ref, out_ref):
    x = x_ref[0]
    out_ref[0] = x * lax.rsqrt(jnp.mean(x * x, axis=-1, keepdims=True) + NORM_EPS) * g_ref[...]


def _final_norm(xa, g, n_ctx_tiles):
    b, t, d = xa.shape
    n_lat = t // TILE - n_ctx_tiles
    return pl.pallas_call(
        _final_kernel,
        grid=(b, n_lat),
        in_specs=[pl.BlockSpec((1, TILE, d), lambda i, j: (i, j + n_ctx_tiles, 0)), _const_spec(g.shape)],
        out_specs=pl.BlockSpec((1, TILE, d), lambda i, j: (i, j, 0)),
        out_shape=jax.ShapeDtypeStruct((b, n_lat * TILE, d), F32),
        compiler_params=_params(2),
        name="final_norm",
    )(xa, g)


def _pad_rows(m, rows, offset):
    out = jnp.zeros((rows, m.shape[1]), m.dtype)
    return out.at[offset:offset + m.shape[0]].set(m)


def kernel(x, c, ctx, c_ctx, w_ada, b_ada, g_norm1, g_norm2, w_in, mu_prev, mu_next,
           rw_w0, rw_w2, rw_a0, rw_a2, rw_kk, rw_ka, rw_rk, rw_g2, rw_lnw, rw_lnb,
           cv_dw, cv_db, cv_lnw, cv_lnb, ml_ib, ml_fb, ml_nw,
           p_a, p_b, p_c, w_out, w_mlp1, w_mlp2, g_final):
    batch, seq, d = x.shape
    ctx_len = ctx.shape[1]
    depth = w_ada.shape[0]
    assert d == D_MODEL and ctx_len == TILE and seq % TILE == 0
    n_ctx_tiles = ctx_len // TILE

    ctx_row = batch
    rows = -(-(batch + 1) // SUBLANES) * SUBLANES
    c_all = jnp.zeros((rows, d), F32).at[:batch].set(c).at[ctx_row].set(c_ctx)
    mods = _ada_mods(c_all, w_ada, b_ada)

    lane = jnp.arange(GW)
    mask_bd = (lane[:, None] // RW_N == lane[None, :] // RW_N).astype(BF16)
    ones_bd = mask_bd

    xa = jnp.concatenate([ctx, x], axis=1)
    for l in range(depth):
        mod = mods[l].reshape(rows, 1, N_MOD * d)
        g1 = g_norm1[l].reshape(1, d)
        g2 = g_norm2[l].reshape(1, d)
        wl = w_in[l]
        gate_w = _pad_rows(wl[:, OFF_ML + 4 * ML_W:OFF_GATE].T, MG_COLS, 0).T
        w_rec = jnp.concatenate([wl[:, :RW_COLS], wl[:, OFF_ML:OFF_ML + 4 * ML_W], gate_w],
                                axis=1).astype(BF16)
        u_rw, u_ml, u_mg = _proj_rec(xa, mod, g1, w_rec, ctx_row, n_ctx_tiles)

        rw = {
            "mu_p": mu_prev[l].reshape(1, RW_COLS), "mu_n": mu_next[l].reshape(1, RW_COLS),
            "w0": rw_w0[l].reshape(2, 1, RW_W), "a0": rw_a0[l].reshape(2, 1, RW_W),
            "w2": jnp.stack([_pad_rows(rw_w2[l, i], 2 * LORA_DECAY, LORA_DECAY * i)
                             for i in range(2)]).astype(BF16),
            "a2": jnp.stack([_pad_rows(rw_a2[l, i], 2 * LORA_ICLR, LORA_ICLR * i)
                             for i in range(2)]).astype(BF16),
            "kk": rw_kk[l].reshape(1, RW_W), "ka": rw_ka[l].reshape(1, RW_W),
            "rk": rw_rk[l].reshape(1, RW_W), "g2": rw_g2[l].astype(BF16),
            "lnw": rw_lnw[l].reshape(1, RW_W), "lnb": rw_lnb[l].reshape(1, RW_W),
            "ones_bd": ones_bd, "mask_bd": mask_bd,
        }
        ya = _rwkv_mixer(u_rw, rw, ctx_len)

        gate_bias = jnp.zeros((1, MG_COLS), F32).at[0, :4 * ML_H].set(
            jnp.stack([ml_ib[l], ml_fb[l]], axis=1).reshape(-1))
        h_b = _mlstm_pass(u_ml, u_mg, gate_bias, True, ctx_len)
        h_f = _mlstm_pass(u_ml, u_mg, gate_bias, False, ctx_len)

        cv = {"w_cv": wl[:, OFF_CV:OFF_ML].astype(BF16), "dw": cv_dw[l],
              "db": cv_db[l].reshape(1, CONV_W), "lnw": cv_lnw[l].reshape(1, CONV_W),
              "lnb": cv_lnb[l].reshape(1, CONV_W)}
        yb = _conv_branch(xa, mod, g1, cv, ctx_row, n_ctx_tiles)

        mg = {"w_gate": wl[:, OFF_GATE:].astype(BF16), "p_a": p_a[l].astype(BF16),
              "p_b": p_b[l].astype(BF16), "p_c": p_c[l].astype(BF16), "w_out": w_out[l].astype(BF16)}
        xa = _merge(xa, mod, g1, ya, yb, h_f, h_b, u_ml, ml_nw[l].reshape(1, ML_W), mg, ctx_row, n_ctx_tiles)
        xa = _mlp(xa, mod, g2, w_mlp1[l].astype(BF16), w_mlp2[l].astype(BF16), ctx_row, n_ctx_tiles)
    return _final_norm(xa, g_final.reshape(1, d), n_ctx_tiles)
```

```python
import functools

import jax
import jax.numpy as jnp
from jax import lax
from jax.experimental import pallas as pl
from jax.experimental.pallas import tpu as pltpu

F32 = jnp.float32
BF16 = jnp.bfloat16

D_MODEL = 1024
GRID_W = 64
N_MOD = 6
NORM_EPS = 1e-6
LN_EPS = 1e-5
RW_N = 64
RW_W = D_MODEL // 2
RW_H = RW_W // RW_N
LORA_DECAY = 64
LORA_ICLR = 64
LORA_GATE = 128
RW_GN_EPS = 64e-5
CONV_W = D_MODEL // 2
CONV_K = 31
ML_H = 4
ML_W = D_MODEL // 2
ML_DH = ML_W // ML_H
N_BRANCH = 3
D_FF = 4 * D_MODEL
RW_COLS = 3 * RW_W + 2 * LORA_DECAY + 2 * LORA_ICLR + LORA_GATE
CV_COLS = 2 * CONV_W
ML_COLS = 4 * ML_W + 4 * ML_H
GATE_COLS = N_BRANCH * D_MODEL
OFF_CV = RW_COLS
OFF_ML = OFF_CV + CV_COLS
OFF_GATE = OFF_ML + ML_COLS

LANES = 128
SUBLANES = 8
TILE = 256
CHUNK = 64
HGROUP = 4
GW = HGROUP * RW_N
N_GROUPS = RW_H // HGROUP
MG_COLS = LANES
CONV_PAD = 16
VMEM_LIMIT = 56 * 1024 * 1024


def _mm(a, b):
    return jnp.dot(a.astype(BF16), b.astype(BF16), preferred_element_type=F32)


def _mm_nt(a, b):
    return lax.dot_general(a.astype(BF16), b.astype(BF16), (((1,), (1,)), ((), ())),
                           preferred_element_type=F32)


def _mm_tn(a, b):
    return lax.dot_general(a.astype(BF16), b.astype(BF16), (((0,), (0,)), ((), ())),
                           preferred_element_type=F32)


def _split3(x):
    hi = x.astype(BF16)
    r1 = x - hi.astype(F32)
    mid = r1.astype(BF16)
    lo = (r1 - mid.astype(F32)).astype(BF16)
    return hi, mid, lo


def _mm_exact_lhs(t, x):
    tb = t.astype(BF16)
    hi, mid, lo = _split3(x)
    dot = functools.partial(jnp.dot, preferred_element_type=F32)
    return dot(tb, hi) + dot(tb, mid) + dot(tb, lo)


def _mm_exact_rhs_nt(x, t):
    tb = t.astype(BF16)
    hi, mid, lo = _split3(x)
    dn = (((1,), (1,)), ((), ()))
    dot = functools.partial(lax.dot_general, dimension_numbers=dn, preferred_element_type=F32)
    return dot(hi, tb) + dot(mid, tb) + dot(lo, tb)


def _segsum(x, ones_bd):
    hi = x.astype(BF16)
    lo = (x - hi.astype(F32)).astype(BF16)
    dot = functools.partial(jnp.dot, preferred_element_type=F32)
    gw = ones_bd.shape[0]
    return jnp.concatenate(
        [dot(hi[:, i:i + gw], ones_bd) + dot(lo[:, i:i + gw], ones_bd) for i in range(0, x.shape[1], gw)],
        axis=1)


def _sigmoid(x):
    return 1.0 / (1.0 + jnp.exp(-x))


def _softplus(x):
    return jnp.maximum(x, 0.0) + jnp.log(1.0 + jnp.exp(-jnp.abs(x)))


def _norm_mod(x, g, shift, scale):
    y = x * lax.rsqrt(jnp.mean(x * x, axis=-1, keepdims=True) + NORM_EPS)
    return (y * g) * (1.0 + scale) + shift


def _layernorm_lanes(x, eps):
    xc = x - jnp.mean(x, axis=-1, keepdims=True)
    return xc * lax.rsqrt(jnp.mean(xc * xc, axis=-1, keepdims=True) + eps)


def _const_spec(shape):
    nd = len(shape)
    return pl.BlockSpec(shape, lambda *_: (0,) * nd)


def _params(n_axes, arbitrary_last=False):
    sem = ["parallel"] * n_axes
    if arbitrary_last:
        sem[-1] = "arbitrary"
    return pltpu.CompilerParams(dimension_semantics=tuple(sem), vmem_limit_bytes=VMEM_LIMIT)


def _ada_kernel(c_ref, w_ref, b_ref, o_ref):
    c = c_ref[...]
    o_ref[0] = _mm(c * _sigmoid(c), w_ref[0]) + b_ref[0]


def _ada_mods(c_all, w_ada, b_ada):
    depth, d, nd = w_ada.shape
    rows = c_all.shape[0]
    return pl.pallas_call(
        _ada_kernel,
        grid=(depth, nd // d),
        in_specs=[pl.BlockSpec((rows, d), lambda l, j: (0, 0)),
                  pl.BlockSpec((1, d, d), lambda l, j: (l, 0, j)),
                  pl.BlockSpec((1, 1, d), lambda l, j: (l, 0, j))],
        out_specs=pl.BlockSpec((1, rows, d), lambda l, j: (l, 0, j)),
        out_shape=jax.ShapeDtypeStruct((depth, rows, nd), F32),
        compiler_params=_params(2),
        name="ada_mods",
    )(c_all, w_ada, b_ada.reshape(depth, 1, nd))


def _tile_spec(width):
    return pl.BlockSpec((1, TILE, width), lambda b, j: (b, j, 0))


def _mod_spec(ctx_row, n_ctx_tiles):
    return pl.BlockSpec((1, 1, N_MOD * D_MODEL),
                        lambda b, j: (jnp.where(j < n_ctx_tiles, ctx_row, b), 0, 0))


def _chunk_index(s, rev, nc_ctx, nc_tot):
    if not rev:
        return s
    return jnp.where(s < nc_ctx, nc_ctx - 1 - s, nc_tot - 1 - (s - nc_ctx))


INV_BASE = 8


def _unit_tri_inverse(l_ms, bd, eye, t_id, i_id):
    def same_block(n):
        sh = n.bit_length() - 1
        return lax.shift_right_logical(t_id, sh) == lax.shift_right_logical(i_id, sh)

    diag = same_block(INV_BASE)
    l_d = [jnp.where(diag, l_m, 0.0) for l_m in l_ms]
    a = [eye - m for m in l_d]
    rows = l_ms[0].shape[0]
    p2 = [_mm(m, bd(m)) for m in l_d]
    ap = [_mm(jnp.concatenate([x, p], axis=0), bd(p)) for x, p in zip(a, p2)]
    t_m = [x + y[:rows] for x, y in zip(a, ap)]
    t_m = [t + _mm(t, bd(y[rows:])) for t, y in zip(t_m, ap)]
    n = INV_BASE
    while n < CHUNK:
        sub = jnp.logical_and(same_block(2 * n), jnp.logical_not(same_block(n)))
        g_m = [_mm(jnp.where(sub, l_m, 0.0), bd(t)) for l_m, t in zip(l_ms, t_m)]
        t_m = [t - _mm(t, bd(g)) for t, g in zip(t_m, g_m)]
        n *= 2
    return t_m


RW_TILE = 4 * CHUNK
RW_SCAN_BATCH = 8


def _shifted_tile(u_ref, up_ref, un_ref, buf_ref, mup_ref, mun_ref, is_first, is_last, rows, col_ranges):
    hs = SUBLANES
    buf_ref[hs:hs + rows, :] = u_ref[0]
    buf_ref[hs - 1:hs, :] = jnp.where(is_first, 0.0, up_ref[0, hs - 1:hs, :])
    buf_ref[hs + rows:hs + rows + 1, :] = jnp.where(is_last, 0.0, un_ref[0, 0:1, :])
    outs = []
    for c0, c1 in col_ranges:
        u = u_ref[0, :, c0:c1]
        prev = buf_ref[hs - 1:hs - 1 + rows, c0:c1]
        nxt = buf_ref[hs + 1:hs + 1 + rows, c0:c1]
        outs.append(u + mup_ref[:, c0:c1] * (prev - u) + mun_ref[:, c0:c1] * (nxt - u))
    return outs


def _halo_specs(rows, n8):
    per8 = rows // SUBLANES
    return [pl.BlockSpec((1, rows, RW_COLS), lambda i, j: (i, j, 0)),
            pl.BlockSpec((1, SUBLANES, RW_COLS), lambda i, j: (i, jnp.maximum(j * per8 - 1, 0), 0)),
            pl.BlockSpec((1, SUBLANES, RW_COLS), lambda i, j: (i, jnp.minimum((j + 1) * per8, n8 - 1), 0))]


def _rwkv_pre_kernel(u_ref, up_ref, un_ref, mup_ref, mun_ref, w0_ref, w2_ref, a0_ref, a2_ref, kk_ref, ka_ref,
                     ones_ref, mbd_ref, y0_ref, rt_ref, gm_ref, dm_ref, wt_ref, buf_ref, *, ctx_tiles, n_tiles):
    j = pl.program_id(1)
    is_first = jnp.logical_or(j == 0, j == ctx_tiles)
    is_last = jnp.logical_or(j == ctx_tiles - 1, j == n_tiles - 1)
    w = RW_W
    lora0 = 3 * w
    lora1 = lora0 + 2 * LORA_DECAY + 2 * LORA_ICLR
    ush, = _shifted_tile(u_ref, up_ref, un_ref, buf_ref, mup_ref, mun_ref, is_first, is_last, RW_TILE,
                         [(0, lora1)])
    r = ush[:, 0:w]
    k = ush[:, w:2 * w]
    v = ush[:, 2 * w:3 * w]
    wd = jnp.tanh(ush[:, lora0:lora0 + 2 * LORA_DECAY])
    ad = ush[:, lora0 + 2 * LORA_DECAY:lora1]
    kkr = k * kk_ref[...]
    kk = kkr / jnp.maximum(jnp.sqrt(_segsum(kkr * kkr, ones_ref[...])), 1e-12)

    row = lax.broadcasted_iota(jnp.int32, (CHUNK, CHUNK), 0)
    col = lax.broadcasted_iota(jnp.int32, (CHUNK, CHUNK), 1)
    t_id = lax.broadcasted_iota(jnp.int32, (CHUNK, GW), 0)
    i_id = jnp.bitwise_and(lax.broadcasted_iota(jnp.int32, (CHUNK, GW), 1), RW_N - 1)
    eye = jnp.where(i_id == t_id, 1.0, 0.0)
    mbd = mbd_ref[...]
    mbd_f2 = jnp.concatenate([mbd, mbd], axis=1).astype(F32)
    zeros = jnp.zeros((CHUNK, GW), F32)

    def bd(m):
        mb = m.astype(BF16)
        return jnp.concatenate([mb] * HGROUP, axis=0) * mbd

    logw_d, k_d_d, kka_d = [], [], []
    for d in range(2):
        w_log = -_softplus(-(w0_ref[d] + _mm(wd, w2_ref[d]))) - 0.5
        logw_d.append(-jnp.exp(w_log))
        a = _sigmoid(a0_ref[d] + _mm(ad, a2_ref[d]))
        k_d_d.append(k * (1.0 + (a - 1.0) * ka_ref[...]))
        kka_d.append(kk * a)
    units = [(d, ci) for d in range(2) for ci in range(RW_TILE // CHUNK)]
    rows_of = lambda ci: slice(ci * CHUNK, (ci + 1) * CHUNK)
    tris = [jnp.where((col >= row) if d == 1 else (col <= row), 1.0, 0.0) for d in range(2)]
    cums = [_mm_exact_lhs(tris[d], logw_d[d][rows_of(ci)]) for d, ci in units]

    chains = []
    for (d, ci), c in zip(units, cums):
        rev = d == 1
        rows = rows_of(ci)
        logw = logw_d[d][rows]
        k_d = k_d_d[d][rows]
        kka = kka_d[d][rows]
        last = 0 if rev else CHUNK - 1
        c_tot = c[last:last + 1, :]
        e_neg = jnp.exp(-c)
        e_rem = jnp.exp(c_tot - c)
        alpha = kk[rows] * jnp.exp(c - logw)
        beta = kka * e_neg
        kappa = k_d * e_neg
        rho = r[rows] * jnp.exp(c)
        kap_hat = k_d * e_rem
        bet_hat = kka * e_rem
        wt_ref[d, 0, ci] = jnp.exp(c_tot)
        strict = (i_id > t_id) if rev else (i_id < t_id)
        nonstrict = (i_id >= t_id) if rev else (i_id <= t_id)
        for g in range(N_GROUPS):
            sl = slice(GW * g, GW * (g + 1))
            chains.append(dict(alpha=alpha[:, sl], beta=beta[:, sl], kappa=kappa[:, sl], rho=rho[:, sl],
                               kap_hat=kap_hat[:, sl], bet_hat=bet_hat[:, sl], v=v[rows, sl],
                               strict=strict, nonstrict=nonstrict))

    ar = [jnp.concatenate([ch["alpha"], ch["rho"]], axis=0) for ch in chains]
    ab = [_mm_nt(x, bd(ch["beta"])) for x, ch in zip(ar, chains)]
    ak = [_mm_nt(x, bd(ch["kappa"])) for x, ch in zip(ar, chains)]
    l_m = [jnp.where(ch["strict"], x[:CHUNK], 0.0) for x, ch in zip(ab, chains)]
    p_b = [jnp.where(ch["nonstrict"], x[CHUNK:], 0.0) for x, ch in zip(ab, chains)]
    akm = [jnp.concatenate([jnp.where(ch["strict"], x[:CHUNK], 0.0),
                            jnp.where(ch["nonstrict"], x[CHUNK:], 0.0)], axis=0) for x, ch in zip(ak, chains)]
    akv = [_mm(x, bd(ch["v"])) for x, ch in zip(akm, chains)]
    t_m = _unit_tri_inverse(l_m, bd, eye, t_id, i_id)
    tu = [_mm(t, jnp.concatenate([bd(x[:CHUNK]), bd(ch["alpha"])], axis=1)) for t, x, ch in zip(t_m, akv, chains)]
    pb = [_mm(p, jnp.concatenate([bd(x[:, :GW]), bd(x[:, GW:])], axis=1)) for p, x in zip(p_b, tu)]
    y0 = [x[CHUNK:] - p[:, :GW] for x, p in zip(akv, pb)]
    rt = [ch["rho"] - p[:, GW:] for ch, p in zip(chains, pb)]
    full = [_mm_tn(jnp.concatenate([ch["v"], x[:, :GW], x[:, GW:]], axis=0),
                   jnp.concatenate(
                       [jnp.concatenate([ch["kap_hat"], -ch["bet_hat"], zeros], axis=0),
                        jnp.concatenate([zeros, zeros, ch["bet_hat"]], axis=0)], axis=1)) * mbd_f2
            for ch, x in zip(chains, tu)]
    blk = [sum(x[RW_N * h:RW_N * (h + 1)] for h in range(1, HGROUP)) + x[0:RW_N] for x in full]

    for ui, (d, ci) in enumerate(units):
        rows = rows_of(ci)
        sel = range(ui * N_GROUPS, (ui + 1) * N_GROUPS)
        y0_ref[d, 0, rows, :] = jnp.concatenate([y0[i] for i in sel], axis=1)
        rt_ref[d, 0, rows, :] = jnp.concatenate([rt[i] for i in sel], axis=1).astype(BF16)
        gm_ref[d, 0, rows, :] = jnp.concatenate([blk[i][:, GW:] for i in sel], axis=1).astype(BF16)
        dm_ref[d, 0, rows, :] = jnp.concatenate([blk[i][:, :GW] for i in sel], axis=1)


def _rwkv_pre(u_rw, prm, ctx_len):
    b, t, _ = u_rw.shape
    n_tiles = t // RW_TILE
    nc = t // CHUNK
    cpt = RW_TILE // CHUNK
    consts = [prm["mu_p"], prm["mu_n"], prm["w0"], prm["w2"], prm["a0"], prm["a2"], prm["kk"], prm["ka"],
              prm["ones_bd"], prm["mask_bd"]]
    tile_out = pl.BlockSpec((2, 1, RW_TILE, RW_W), lambda i, j: (0, i, j, 0))
    return pl.pallas_call(
        functools.partial(_rwkv_pre_kernel, ctx_tiles=ctx_len // RW_TILE, n_tiles=n_tiles),
        grid=(b, n_tiles),
        in_specs=_halo_specs(RW_TILE, t // SUBLANES) + [_const_spec(c.shape) for c in consts],
        out_specs=[tile_out, tile_out, tile_out, tile_out,
                   pl.BlockSpec((2, 1, cpt, 1, RW_W), lambda i, j: (0, i, j, 0, 0))],
        out_shape=[jax.ShapeDtypeStruct((2, b, t, RW_W), F32),
                   jax.ShapeDtypeStruct((2, b, t, RW_W), BF16),
                   jax.ShapeDtypeStruct((2, b, t, RW_W), BF16),
                   jax.ShapeDtypeStruct((2, b, t, RW_W), F32),
                   jax.ShapeDtypeStruct((2, b, nc, 1, RW_W), F32)],
        scratch_shapes=[pltpu.VMEM((RW_TILE + 2 * SUBLANES, RW_COLS), F32)],
        compiler_params=_params(2),
        name="rwkv_pre",
    )(u_rw, u_rw, u_rw, *consts)


def _rwkv_scan_kernel(y0_ref, rt_ref, gm_ref, dm_ref, wt_ref, mbd_ref, y_ref, st_ref, *, nb):
    @pl.when(pl.program_id(2) == 0)
    def _():
        st_ref[...] = jnp.zeros_like(st_ref)

    mbd = mbd_ref[...]

    def bd(mb):
        return jnp.concatenate([mb] * HGROUP, axis=0) * mbd

    for bi in range(nb):
        ys = []
        for g in range(N_GROUPS):
            sl = slice(GW * g, GW * (g + 1))
            s0 = st_ref[bi, g]
            s0b = s0.astype(BF16)
            ys.append(y0_ref[0, bi, :, sl] + lax.dot_general(
                rt_ref[0, bi, :, sl], bd(s0b), (((1,), (1,)), ((), ())), preferred_element_type=F32))
            st_ref[bi, g] = (s0 * wt_ref[0, bi, 0, :, sl]
                             - jnp.dot(s0b, bd(gm_ref[0, bi, :, sl]), preferred_element_type=F32)
                             + dm_ref[0, bi, :, sl])
        y_ref[0, bi] = jnp.concatenate(ys, axis=1)


def _rwkv_scan(y0, rt, gm, dm, wt, mask_bd, nc_ctx):
    _, b, t, _ = y0.shape
    nc_tot = t // CHUNK
    nb = RW_SCAN_BATCH if b % RW_SCAN_BATCH == 0 else 1

    def cidx(d, s):
        rev = jnp.where(s < nc_ctx, nc_ctx - 1 - s, nc_tot - 1 - (s - nc_ctx))
        return jnp.where(d == 1, rev, s)

    chunk = pl.BlockSpec((1, nb, CHUNK, RW_W), lambda d, i, s: (d, i, cidx(d, s), 0))
    return pl.pallas_call(
        functools.partial(_rwkv_scan_kernel, nb=nb),
        grid=(2, b // nb, nc_tot),
        in_specs=[chunk, chunk, chunk, chunk,
                  pl.BlockSpec((1, nb, 1, 1, RW_W), lambda d, i, s: (d, i, cidx(d, s), 0, 0)),
                  _const_spec(mask_bd.shape)],
        out_specs=chunk,
        out_shape=jax.ShapeDtypeStruct(y0.shape, F32),
        scratch_shapes=[pltpu.VMEM((nb, N_GROUPS, RW_N, GW), F32)],
        compiler_params=_params(3, arbitrary_last=True),
        name="rwkv_scan",
    )(y0, rt, gm, dm, wt, mask_bd)


def _rwkv_out_kernel(u_ref, up_ref, un_ref, y_ref, mup_ref, mun_ref, rk_ref, g2_ref, lnw_ref, lnb_ref, ones_ref,
                     out_ref, buf_ref, *, ctx_tiles, n_tiles):
    j = pl.program_id(1)
    is_first = jnp.logical_or(j == 0, j == ctx_tiles)
    is_last = jnp.logical_or(j == ctx_tiles - 1, j == n_tiles - 1)
    w = RW_W
    rkv, gd = _shifted_tile(u_ref, up_ref, un_ref, buf_ref, mup_ref, mun_ref, is_first, is_last, TILE,
                            [(0, 3 * w), (RW_COLS - LORA_GATE, RW_COLS)])
    r = rkv[:, 0:w]
    k = rkv[:, w:2 * w]
    v = rkv[:, 2 * w:3 * w]
    ones_bd = ones_ref[...]
    y = y_ref[0, 0] + y_ref[1, 0]
    mean = _segsum(y, ones_bd) * (1.0 / RW_N)
    yc = y - mean
    var = _segsum(yc * yc, ones_bd) * (1.0 / RW_N)
    yn = yc * lax.rsqrt(var + RW_GN_EPS) * lnw_ref[...] + lnb_ref[...]
    bonus = _segsum(r * k * rk_ref[...], ones_bd) * v
    out_ref[0] = (yn + bonus) * _mm(_sigmoid(gd), g2_ref[...])


def _rwkv_out(u_rw, y, prm, ctx_len):
    b, t, _ = u_rw.shape
    n_tiles = t // TILE
    consts = [prm["mu_p"], prm["mu_n"], prm["rk"], prm["g2"], prm["lnw"], prm["lnb"], prm["ones_bd"]]
    return pl.pallas_call(
        functools.partial(_rwkv_out_kernel, ctx_tiles=ctx_len // TILE, n_tiles=n_tiles),
        grid=(b, n_tiles),
        in_specs=_halo_specs(TILE, t // SUBLANES)
        + [pl.BlockSpec((2, 1, TILE, RW_W), lambda i, j: (0, i, j, 0))]
        + [_const_spec(c.shape) for c in consts],
        out_specs=_tile_spec(RW_W),
        out_shape=jax.ShapeDtypeStruct((b, t, RW_W), F32),
        scratch_shapes=[pltpu.VMEM((TILE + 2 * SUBLANES, RW_COLS), F32)],
        compiler_params=_params(2),
        name="rwkv_out",
    )(u_rw, u_rw, u_rw, y, *consts)


def _rwkv_mixer(u_rw, prm, ctx_len):
    y0, rt, gm, dm, wt = _rwkv_pre(u_rw, prm, ctx_len)
    y = _rwkv_scan(y0, rt, gm, dm, wt, prm["mask_bd"], ctx_len // CHUNK)
    return _rwkv_out(u_rw, y, prm, ctx_len)


ML_BATCH = 8
ML_CHUNK = 128


def _mlstm_kernel(u_ref, g_ref, bias_ref, out_ref, c_ref, n_ref, m_ref, *, rev, d, nb):
    s = pl.program_id(1)

    @pl.when(s == 0)
    def _():
        c_ref[...] = jnp.zeros_like(c_ref)
        n_ref[...] = jnp.zeros_like(n_ref)
        m_ref[...] = jnp.zeros_like(m_ref)

    lane = lax.broadcasted_iota(jnp.int32, (ML_CHUNK, MG_COLS), 1)
    is_forget = jnp.bitwise_and(lane, ML_H) != 0
    row = lax.broadcasted_iota(jnp.int32, (ML_CHUNK, ML_CHUNK), 0)
    col = lax.broadcasted_iota(jnp.int32, (ML_CHUNK, ML_CHUNK), 1)
    seen = (col >= row) if rev else (col <= row)
    tri = jnp.where(seen, 1.0, 0.0)
    last = 0 if rev else ML_CHUNK - 1

    us = [u_ref[bi] for bi in range(nb)]
    gps = [g_ref[bi] + bias_ref[...] for bi in range(nb)]
    gates = [jnp.where(is_forget, -_softplus(-gp), gp) for gp in gps]
    gates_t = [g.T for g in gates]
    cum_col = [_mm_exact_lhs(tri, g) for g in gates]
    cum_row = [_mm_exact_rhs_nt(g, tri) for g in gates_t]

    chains = [(bi, h) for bi in range(nb) for h in range(ML_H)]
    jf = lambda h: 2 * ML_H * d + ML_H + h
    ji = lambda h: 2 * ML_H * d + h
    q = [us[bi][:, ML_DH * h:ML_DH * (h + 1)] for bi, h in chains]
    k = [us[bi][:, ML_W + ML_DH * h:ML_W + ML_DH * (h + 1)] * (ML_DH ** -0.5) for bi, h in chains]
    v = [us[bi][:, 2 * ML_W + ML_DH * h:2 * ML_W + ML_DH * (h + 1)] for bi, h in chains]
    b_col = [cum_col[bi][:, jf(h):jf(h) + 1] for bi, h in chains]
    b_row = [cum_row[bi][jf(h):jf(h) + 1, :] for bi, h in chains]
    i_col = [gates[bi][:, ji(h):ji(h) + 1] for bi, h in chains]
    i_row = [gates_t[bi][ji(h):ji(h) + 1, :] for bi, h in chains]
    c_st = [c_ref[bi * ML_H + h] for bi, h in chains]
    n_st = [n_ref[bi * ML_H + h] for bi, h in chains]
    m_st = [m_ref[bi * ML_H + h][:, 0:1] for bi, h in chains]
    n_ch = range(len(chains))

    qk = [_mm_nt(q[i], k[i]) for i in n_ch]
    qc = [_mm_nt(q[i], c_st[i]) for i in n_ch]
    dlog = [jnp.where(seen, b_col[i] - b_row[i] + i_row[i], -jnp.inf) for i in n_ch]
    inter = [b_col[i] + m_st[i] for i in n_ch]
    m_t = [jnp.maximum(inter[i], jnp.max(dlog[i], axis=1, keepdims=True)) for i in n_ch]
    s_mat = [qk[i] * jnp.exp(dlog[i] - m_t[i]) for i in n_ch]
    sv = [_mm(s_mat[i], v[i]) for i in n_ch]
    b_last = [b_col[i][last:last + 1, :] for i in n_ch]
    wlog = [b_last[i] - b_col[i] + i_col[i] for i in n_ch]
    m_new = [jnp.maximum(b_last[i] + m_st[i], jnp.max(wlog[i], axis=0, keepdims=True)) for i in n_ch]
    wk = [jnp.exp(wlog[i] - m_new[i]) * k[i] for i in n_ch]
    vwk = [_mm_tn(v[i], wk[i]) for i in n_ch]

    hs = []
    for i, (bi, h) in enumerate(chains):
        w_inter = jnp.exp(inter[i] - m_t[i])
        num = w_inter * qc[i] + sv[i]
        den = (w_inter * jnp.sum(q[i] * n_st[i], axis=1, keepdims=True)
               + jnp.sum(s_mat[i], axis=1, keepdims=True))
        hs.append(num / jnp.maximum(jnp.abs(den), jnp.exp(-m_t[i])))
        carry_w = jnp.exp(b_last[i] + m_st[i] - m_new[i])
        st = bi * ML_H + h
        c_ref[st] = carry_w * c_st[i] + vwk[i]
        n_ref[st] = carry_w * n_st[i] + jnp.sum(wk[i], axis=0, keepdims=True)
        m_ref[st] = jnp.broadcast_to(m_new[i], (1, LANES))

    for bi in range(nb):
        out_ref[bi] = jnp.concatenate(hs[bi * ML_H:(bi + 1) * ML_H], axis=1)


def _mlstm_pass(u_ml, u_mg, gate_bias, rev, ctx_len):
    b, t, _ = u_ml.shape
    nc_tot = t // ML_CHUNK
    nb = ML_BATCH if b % ML_BATCH == 0 else 1
    cidx = functools.partial(_chunk_index, rev=rev, nc_ctx=ctx_len // ML_CHUNK, nc_tot=nc_tot)
    in_specs = [pl.BlockSpec((nb, ML_CHUNK, 3 * ML_W), lambda i, s: (i, cidx(s), 0)),
                pl.BlockSpec((nb, ML_CHUNK, MG_COLS), lambda i, s: (i, cidx(s), 0)),
                _const_spec(gate_bias.shape)]
    return pl.pallas_call(
        functools.partial(_mlstm_kernel, rev=rev, d=1 if rev else 0, nb=nb),
        grid=(b // nb, nc_tot),
        in_specs=in_specs,
        out_specs=pl.BlockSpec((nb, ML_CHUNK, ML_W), lambda i, s: (i, cidx(s), 0)),
        out_shape=jax.ShapeDtypeStruct((b, t, ML_W), F32),
        scratch_shapes=[pltpu.VMEM((nb * ML_H, ML_DH, ML_DH), F32),
                        pltpu.VMEM((nb * ML_H, 1, ML_DH), F32),
                        pltpu.VMEM((nb * ML_H, 1, LANES), F32)],
        compiler_params=_params(2, arbitrary_last=True),
        name="mlstm_bwd" if rev else "mlstm_fwd",
    )(u_ml, u_mg, gate_bias)


def _conv_kernel(x_ref, mod_ref, g_ref, w_ref, wrec_ref, dw_ref, db_ref, lnw_ref, lnb_ref,
                 urw_ref, uml_ref, umg_ref, out_ref, *, n_ctx_tiles):
    d = D_MODEL
    mod = mod_ref[0]
    h = _norm_mod(x_ref[0], g_ref[...], mod[:, 0:d], mod[:, d:2 * d]).astype(BF16)
    u = _mm(h, w_ref[...])
    z = u[:, :CONV_W] * _sigmoid(u[:, CONV_W:])
    dw = dw_ref[...]
    half = CONV_K // 2

    n_blk = TILE // GRID_W
    rec_cols = wrec_ref.shape[1]
    outs = [(0, RW_COLS, urw_ref), (RW_COLS, RW_COLS + 4 * ML_W, uml_ref), (RW_COLS + 4 * ML_W, rec_cols, umg_ref)]

    def slab(part):
        return part * rec_cols // n_blk, (part + 1) * rec_cols // n_blk

    def project(part):
        c0, c1 = slab(part)
        return _mm(h, wrec_ref[:, c0:c1])

    def store_slab(part, u_rec):
        c0, c1 = slab(part)
        for o0, o1, ref in outs:
            a, b = max(c0, o0), min(c1, o1)
            if a < b:
                ref[0, :, a - o0:b - o0] = u_rec[:, a - c0:b - c0]

    def conv_segments(seg_len):
        lead = CONV_PAD - half
        win_rows = GRID_W + 2 * CONV_PAD
        for blk in range(n_blk):
            u_rec = project(blk)
            row0 = blk * GRID_W
            seg_start = row0 // seg_len * seg_len
            lo, hi = row0 - CONV_PAD, row0 + GRID_W + CONV_PAD
            parts = [z[max(lo, seg_start):min(hi, seg_start + seg_len)]]
            if lo < seg_start:
                parts.insert(0, jnp.zeros((seg_start - lo, CONV_W), F32))
            if hi > seg_start + seg_len:
                parts.append(jnp.zeros((hi - seg_start - seg_len, CONV_W), F32))
            win = jnp.concatenate(parts, axis=0) if len(parts) > 1 else parts[0]
            acc = jnp.zeros((GRID_W, CONV_W), F32)
            for res in range(SUBLANES):
                rot = win if res == 0 else pltpu.roll(win, win_rows - res, 0)
                for q in range((lead + CONV_K - 1 - res) // SUBLANES + 1):
                    j = SUBLANES * q + res - lead
                    if 0 <= j < CONV_K:
                        acc = acc + dw[j:j + 1, :] * rot[SUBLANES * q:SUBLANES * q + GRID_W]
            y = _layernorm_lanes(acc + db_ref[...], LN_EPS) * lnw_ref[...] + lnb_ref[...]
            out_ref[0, row0:row0 + GRID_W, :] = y * _sigmoid(y)
            store_slab(blk, u_rec)

    is_ctx = pl.program_id(1) < n_ctx_tiles

    @pl.when(is_ctx)
    def _():
        conv_segments(TILE)

    @pl.when(jnp.logical_not(is_ctx))
    def _():
        conv_segments(GRID_W)


def _proj_conv(xa, mod, g, w_rec, prm, ctx_row, n_ctx_tiles):
    b, t, d = xa.shape
    consts = [g, prm["w_cv"], w_rec, prm["dw"], prm["db"], prm["lnw"], prm["lnb"]]
    widths = [RW_COLS, 4 * ML_W, MG_COLS, CONV_W]
    return pl.pallas_call(
        functools.partial(_conv_kernel, n_ctx_tiles=n_ctx_tiles),
        grid=(b, t // TILE),
        in_specs=[_tile_spec(d), _mod_spec(ctx_row, n_ctx_tiles)] + [_const_spec(c.shape) for c in consts],
        out_specs=[_tile_spec(w) for w in widths],
        out_shape=[jax.ShapeDtypeStruct((b, t, w), F32) for w in widths],
        compiler_params=_params(2),
        name="proj_conv",
    )(xa, mod, *consts)


def _merge_kernel(x_ref, mod_ref, g_ref, ya_ref, yb_ref, hf_ref, hb_ref, o_ref, nw_ref,
                  wg_ref, pa_ref, pb_ref, pc_ref, wo_ref, out_ref):
    d = D_MODEL
    mod = mod_ref[0]
    x = x_ref[0]
    h = _norm_mod(x, g_ref[...], mod[:, 0:d], mod[:, d:2 * d])
    ug = _sigmoid(_mm(h, wg_ref[...]))
    hsum = hf_ref[0] + hb_ref[0]
    hn = jnp.concatenate([_layernorm_lanes(hsum[:, ML_DH * i:ML_DH * (i + 1)], LN_EPS) for i in range(ML_H)],
                         axis=1)
    yc = hn * nw_ref[...] * _sigmoid(o_ref[0])
    m = (ug[:, 0:d] * _mm(ya_ref[0], pa_ref[...])
         + ug[:, d:2 * d] * _mm(yb_ref[0], pb_ref[...])
         + ug[:, 2 * d:3 * d] * _mm(yc, pc_ref[...]))
    out_ref[0] = x + mod[:, 2 * d:3 * d] * _mm(m, wo_ref[...])


def _merge(xa, mod, g, ya, yb, h_f, h_b, u_ml, nw, prm, ctx_row, n_ctx_tiles):
    b, t, d = xa.shape
    consts = [prm["w_gate"], prm["p_a"], prm["p_b"], prm["p_c"], prm["w_out"]]
    o_cols = 3 * ML_W // ML_W
    return pl.pallas_call(
        _merge_kernel,
        grid=(b, t // TILE),
        in_specs=[_tile_spec(d), _mod_spec(ctx_row, n_ctx_tiles), _const_spec(g.shape),
                  _tile_spec(RW_W), _tile_spec(CONV_W), _tile_spec(ML_W), _tile_spec(ML_W),
                  pl.BlockSpec((1, TILE, ML_W), lambda i, j: (i, j, o_cols)), _const_spec(nw.shape)]
        + [_const_spec(c.shape) for c in consts],
        out_specs=_tile_spec(d),
        out_shape=jax.ShapeDtypeStruct((b, t, d), F32),
        compiler_params=_params(2),
        name="merge",
    )(xa, mod, g, ya, yb, h_f, h_b, u_ml, nw, *consts)


FF_CHUNK = 1024


def _mlp_kernel(x_ref, mod_ref, g_ref, w1_ref, w2_ref, out_ref):
    d = D_MODEL
    mod = mod_ref[0]
    x = x_ref[0]
    h = _norm_mod(x, g_ref[...], mod[:, 3 * d:4 * d], mod[:, 4 * d:5 * d]).astype(BF16)
    acc = jnp.zeros((TILE, d), F32)
    for i in range(D_FF // FF_CHUNK):
        hid = jnp.maximum(_mm(h, w1_ref[:, i * FF_CHUNK:(i + 1) * FF_CHUNK]), 0.0)
        acc = acc + _mm(hid * hid, w2_ref[i * FF_CHUNK:(i + 1) * FF_CHUNK, :])
    out_ref[0] = x + mod[:, 5 * d:6 * d] * acc


def _mlp(xa, mod, g, w1, w2, ctx_row, n_ctx_tiles):
    b, t, d = xa.shape
    return pl.pallas_call(
        _mlp_kernel,
        grid=(b, t // TILE),
        in_specs=[_tile_spec(d), _mod_spec(ctx_row, n_ctx_tiles), _const_spec(g.shape),
                  _const_spec(w1.shape), _const_spec(w2.shape)],
        out_specs=_tile_spec(d),
        out_shape=jax.ShapeDtypeStruct((b, t, d), F32),
        compiler_params=_params(2),
        name="mlp",
    )(xa, mod, g, w1, w2)


def _final_kernel(x_ref, g_ref, out_ref):
    x = x_ref[0]
    out_ref[0] = x * lax.rsqrt(jnp.mean(x * x, axis=-1, keepdims=True) + NORM_EPS) * g_ref[...]


def _final_norm(xa, g, n_ctx_tiles):
    b, t, d = xa.shape
    n_lat = t // TILE - n_ctx_tiles
    return pl.pallas_call(
        _final_kernel,
        grid=(b, n_lat),
        in_specs=[pl.BlockSpec((1, TILE, d), lambda i, j: (i, j + n_ctx_tiles, 0)), _const_spec(g.shape)],
        out_specs=pl.BlockSpec((1, TILE, d), lambda i, j: (i, j, 0)),
        out_shape=jax.ShapeDtypeStruct((b, n_lat * TILE, d), F32),
        compiler_params=_params(2),
        name="final_norm",
    )(xa, g)


def _pad_rows(m, rows, offset):
    out = jnp.zeros((rows, m.shape[1]), m.dtype)
    return out.at[offset:offset + m.shape[0]].set(m)


def kernel(x, c, ctx, c_ctx, w_ada, b_ada, g_norm1, g_norm2, w_in, mu_prev, mu_next,
           rw_w0, rw_w2, rw_a0, rw_a2, rw_kk, rw_ka, rw_rk, rw_g2, rw_lnw, rw_lnb,
           cv_dw, cv_db, cv_lnw, cv_lnb, ml_ib, ml_fb, ml_nw,
           p_a, p_b, p_c, w_out, w_mlp1, w_mlp2, g_final):
    batch, seq, d = x.shape
    ctx_len = ctx.shape[1]
    depth = w_ada.shape[0]
    assert d == D_MODEL and ctx_len == TILE and seq % TILE == 0
    n_ctx_tiles = ctx_len // TILE

    ctx_row = batch
    rows = -(-(batch + 1) // SUBLANES) * SUBLANES
    c_all = jnp.zeros((rows, d), F32).at[:batch].set(c).at[ctx_row].set(c_ctx)
    mods = _ada_mods(c_all, w_ada, b_ada)

    lane = jnp.arange(GW)
    mask_bd = (lane[:, None] // RW_N == lane[None, :] // RW_N).astype(BF16)
    ones_bd = mask_bd

    xa = jnp.concatenate([ctx, x], axis=1)
    for l in range(depth):
        mod = mods[l].reshape(rows, 1, N_MOD * d)
        g1 = g_norm1[l].reshape(1, d)
        g2 = g_norm2[l].reshape(1, d)
        wl = w_in[l]
        gate_w = _pad_rows(wl[:, OFF_ML + 4 * ML_W:OFF_GATE].T, MG_COLS, 0).T
        w_rec = jnp.concatenate([wl[:, :RW_COLS], wl[:, OFF_ML:OFF_ML + 4 * ML_W], gate_w],
                                axis=1).astype(BF16)
        cv = {"w_cv": wl[:, OFF_CV:OFF_ML].astype(BF16), "dw": cv_dw[l],
              "db": cv_db[l].reshape(1, CONV_W), "lnw": cv_lnw[l].reshape(1, CONV_W),
              "lnb": cv_lnb[l].reshape(1, CONV_W)}
        u_rw, u_ml, u_mg, yb = _proj_conv(xa, mod, g1, w_rec, cv, ctx_row, n_ctx_tiles)

        rw = {
            "mu_p": mu_prev[l].reshape(1, RW_COLS), "mu_n": mu_next[l].reshape(1, RW_COLS),
            "w0": rw_w0[l].reshape(2, 1, RW_W), "a0": rw_a0[l].reshape(2, 1, RW_W),
            "w2": jnp.stack([_pad_rows(rw_w2[l, i], 2 * LORA_DECAY, LORA_DECAY * i)
                             for i in range(2)]).astype(BF16),
            "a2": jnp.stack([_pad_rows(rw_a2[l, i], 2 * LORA_ICLR, LORA_ICLR * i)
                             for i in range(2)]).astype(BF16),
            "kk": rw_kk[l].reshape(1, RW_W), "ka": rw_ka[l].reshape(1, RW_W),
            "rk": rw_rk[l].reshape(1, RW_W), "g2": rw_g2[l].astype(BF16),
            "lnw": rw_lnw[l].reshape(1, RW_W), "lnb": rw_lnb[l].reshape(1, RW_W),
            "ones_bd": ones_bd, "mask_bd": mask_bd,
        }
        ya = _rwkv_mixer(u_rw, rw, ctx_len)

        gate_bias = jnp.zeros((1, MG_COLS), F32).at[0, :4 * ML_H].set(
            jnp.stack([ml_ib[l], ml_fb[l]], axis=1).reshape(-1))
        h_b = _mlstm_pass(u_ml, u_mg, gate_bias, True, ctx_len)
        h_f = _mlstm_pass(u_ml, u_mg, gate_bias, False, ctx_len)

        mg = {"w_gate": wl[:, OFF_GATE:].astype(BF16), "p_a": p_a[l].astype(BF16),
              "p_b": p_b[l].astype(BF16), "p_c": p_c[l].astype(BF16), "w_out": w_out[l].astype(BF16)}
        xa = _merge(xa, mod, g1, ya, yb, h_f, h_b, u_ml, ml_nw[l].reshape(1, ML_W), mg, ctx_row, n_ctx_tiles)
        xa = _mlp(xa, mod, g2, w_mlp1[l].astype(BF16), w_mlp2[l].astype(BF16), ctx_row, n_ctx_tiles)
    return _final_norm(xa, g_final.reshape(1, d), n_ctx_tiles)
```

```python
import functools

import jax
import jax.numpy as jnp
from jax import lax
from jax.experimental import pallas as pl
from jax.experimental.pallas import tpu as pltpu

F32 = jnp.float32
BF16 = jnp.bfloat16

D_MODEL = 1024
GRID_W = 64
N_MOD = 6
NORM_EPS = 1e-6
LN_EPS = 1e-5
RW_N = 64
RW_W = D_MODEL // 2
RW_H = RW_W // RW_N
LORA_DECAY = 64
LORA_ICLR = 64
LORA_GATE = 128
RW_GN_EPS = 64e-5
CONV_W = D_MODEL // 2
CONV_K = 31
ML_H = 4
ML_W = D_MODEL // 2
ML_DH = ML_W // ML_H
N_BRANCH = 3
D_FF = 4 * D_MODEL
RW_COLS = 3 * RW_W + 2 * LORA_DECAY + 2 * LORA_ICLR + LORA_GATE
CV_COLS = 2 * CONV_W
ML_COLS = 4 * ML_W + 4 * ML_H
GATE_COLS = N_BRANCH * D_MODEL
OFF_CV = RW_COLS
OFF_ML = OFF_CV + CV_COLS
OFF_GATE = OFF_ML + ML_COLS

LANES = 128
SUBLANES = 8
TILE = 256
CHUNK = 64
HGROUP = 4
GW = HGROUP * RW_N
N_GROUPS = RW_H // HGROUP
MG_COLS = LANES
CONV_PAD = 16
VMEM_LIMIT = 56 * 1024 * 1024


def _mm(a, b):
    return jnp.dot(a.astype(BF16), b.astype(BF16), preferred_element_type=F32)


def _mm_nt(a, b):
    return lax.dot_general(a.astype(BF16), b.astype(BF16), (((1,), (1,)), ((), ())),
                           preferred_element_type=F32)


def _mm_tn(a, b):
    return lax.dot_general(a.astype(BF16), b.astype(BF16), (((0,), (0,)), ((), ())),
                           preferred_element_type=F32)


def _split3(x):
    hi = x.astype(BF16)
    r1 = x - hi.astype(F32)
    mid = r1.astype(BF16)
    lo = (r1 - mid.astype(F32)).astype(BF16)
    return hi, mid, lo


def _mm_exact_lhs(t, x):
    tb = t.astype(BF16)
    hi, mid, lo = _split3(x)
    dot = functools.partial(jnp.dot, preferred_element_type=F32)
    return dot(tb, hi) + dot(tb, mid) + dot(tb, lo)


def _mm_exact_rhs_nt(x, t):
    tb = t.astype(BF16)
    hi, mid, lo = _split3(x)
    dn = (((1,), (1,)), ((), ()))
    dot = functools.partial(lax.dot_general, dimension_numbers=dn, preferred_element_type=F32)
    return dot(hi, tb) + dot(mid, tb) + dot(lo, tb)


def _segsum(x, ones_bd):
    hi = x.astype(BF16)
    lo = (x - hi.astype(F32)).astype(BF16)
    dot = functools.partial(jnp.dot, preferred_element_type=F32)
    gw = ones_bd.shape[0]
    return jnp.concatenate(
        [dot(hi[:, i:i + gw], ones_bd) + dot(lo[:, i:i + gw], ones_bd) for i in range(0, x.shape[1], gw)],
        axis=1)


def _sigmoid(x):
    return 1.0 / (1.0 + jnp.exp(-x))


def _softplus(x):
    return jnp.maximum(x, 0.0) + jnp.log(1.0 + jnp.exp(-jnp.abs(x)))


def _norm_mod(x, g, shift, scale):
    y = x * lax.rsqrt(jnp.mean(x * x, axis=-1, keepdims=True) + NORM_EPS)
    return (y * g) * (1.0 + scale) + shift


def _layernorm_lanes(x, eps):
    xc = x - jnp.mean(x, axis=-1, keepdims=True)
    return xc * lax.rsqrt(jnp.mean(xc * xc, axis=-1, keepdims=True) + eps)


def _const_spec(shape):
    nd = len(shape)
    return pl.BlockSpec(shape, lambda *_: (0,) * nd)


def _params(n_axes, arbitrary_last=False):
    sem = ["parallel"] * n_axes
    if arbitrary_last:
        sem[-1] = "arbitrary"
    return pltpu.CompilerParams(dimension_semantics=tuple(sem), vmem_limit_bytes=VMEM_LIMIT)


def _ada_kernel(c_ref, w_ref, b_ref, o_ref):
    c = c_ref[...]
    o_ref[0] = _mm(c * _sigmoid(c), w_ref[0]) + b_ref[0]


def _ada_mods(c_all, w_ada, b_ada):
    depth, d, nd = w_ada.shape
    rows = c_all.shape[0]
    return pl.pallas_call(
        _ada_kernel,
        grid=(depth, nd // d),
        in_specs=[pl.BlockSpec((rows, d), lambda l, j: (0, 0)),
                  pl.BlockSpec((1, d, d), lambda l, j: (l, 0, j)),
                  pl.BlockSpec((1, 1, d), lambda l, j: (l, 0, j))],
        out_specs=pl.BlockSpec((1, rows, d), lambda l, j: (l, 0, j)),
        out_shape=jax.ShapeDtypeStruct((depth, rows, nd), F32),
        compiler_params=_params(2),
        name="ada_mods",
    )(c_all, w_ada, b_ada.reshape(depth, 1, nd))


def _tile_spec(width):
    return pl.BlockSpec((1, TILE, width), lambda b, j: (b, j, 0))


def _mod_spec(ctx_row, n_ctx_tiles):
    return pl.BlockSpec((1, 1, N_MOD * D_MODEL),
                        lambda b, j: (jnp.where(j < n_ctx_tiles, ctx_row, b), 0, 0))


def _chunk_index(s, rev, nc_ctx, nc_tot):
    if not rev:
        return s
    return jnp.where(s < nc_ctx, nc_ctx - 1 - s, nc_tot - 1 - (s - nc_ctx))


INV_BASE = 8


def _unit_tri_inverse(l_ms, bd, eye, t_id, i_id):
    def same_block(n):
        sh = n.bit_length() - 1
        return lax.shift_right_logical(t_id, sh) == lax.shift_right_logical(i_id, sh)

    diag = same_block(INV_BASE)
    l_d = [jnp.where(diag, l_m, 0.0) for l_m in l_ms]
    a = [eye - m for m in l_d]
    rows = l_ms[0].shape[0]
    p2 = [_mm(m, bd(m)) for m in l_d]
    ap = [_mm(jnp.concatenate([x, p], axis=0), bd(p)) for x, p in zip(a, p2)]
    t_m = [x + y[:rows] for x, y in zip(a, ap)]
    t_m = [t + _mm(t, bd(y[rows:])) for t, y in zip(t_m, ap)]
    n = INV_BASE
    while n < CHUNK:
        sub = jnp.logical_and(same_block(2 * n), jnp.logical_not(same_block(n)))
        g_m = [_mm(jnp.where(sub, l_m, 0.0), bd(t)) for l_m, t in zip(l_ms, t_m)]
        t_m = [t - _mm(t, bd(g)) for t, g in zip(t_m, g_m)]
        n *= 2
    return t_m


RW_TILE = 4 * CHUNK
RW_SCAN_BATCH = 16


def _shifted_tile(u_ref, up_ref, un_ref, buf_ref, mup_ref, mun_ref, is_first, is_last, rows, col_ranges):
    hs = SUBLANES
    buf_ref[hs:hs + rows, :] = u_ref[0]
    buf_ref[hs - 1:hs, :] = jnp.where(is_first, 0.0, up_ref[0, hs - 1:hs, :])
    buf_ref[hs + rows:hs + rows + 1, :] = jnp.where(is_last, 0.0, un_ref[0, 0:1, :])
    outs = []
    for c0, c1 in col_ranges:
        u = u_ref[0, :, c0:c1]
        prev = buf_ref[hs - 1:hs - 1 + rows, c0:c1]
        nxt = buf_ref[hs + 1:hs + 1 + rows, c0:c1]
        outs.append(u + mup_ref[:, c0:c1] * (prev - u) + mun_ref[:, c0:c1] * (nxt - u))
    return outs


def _halo_specs(rows, n8):
    per8 = rows // SUBLANES
    return [pl.BlockSpec((1, rows, RW_COLS), lambda i, j: (i, j, 0)),
            pl.BlockSpec((1, SUBLANES, RW_COLS), lambda i, j: (i, jnp.maximum(j * per8 - 1, 0), 0)),
            pl.BlockSpec((1, SUBLANES, RW_COLS), lambda i, j: (i, jnp.minimum((j + 1) * per8, n8 - 1), 0))]


def _rwkv_pre_kernel(u_ref, up_ref, un_ref, mup_ref, mun_ref, w0_ref, w2_ref, a0_ref, a2_ref, kk_ref, ka_ref,
                     ones_ref, mbd_ref, y0_ref, rt_ref, gm_ref, dm_ref, wt_ref, buf_ref, *, ctx_tiles, n_tiles):
    j = pl.program_id(1)
    is_first = jnp.logical_or(j == 0, j == ctx_tiles)
    is_last = jnp.logical_or(j == ctx_tiles - 1, j == n_tiles - 1)
    w = RW_W
    lora0 = 3 * w
    lora1 = lora0 + 2 * LORA_DECAY + 2 * LORA_ICLR
    ush, = _shifted_tile(u_ref, up_ref, un_ref, buf_ref, mup_ref, mun_ref, is_first, is_last, RW_TILE,
                         [(0, lora1)])
    r = ush[:, 0:w]
    k = ush[:, w:2 * w]
    v = ush[:, 2 * w:3 * w]
    wd = jnp.tanh(ush[:, lora0:lora0 + 2 * LORA_DECAY])
    ad = ush[:, lora0 + 2 * LORA_DECAY:lora1]
    kkr = k * kk_ref[...]
    kk = kkr / jnp.maximum(jnp.sqrt(_segsum(kkr * kkr, ones_ref[...])), 1e-12)

    row = lax.broadcasted_iota(jnp.int32, (CHUNK, CHUNK), 0)
    col = lax.broadcasted_iota(jnp.int32, (CHUNK, CHUNK), 1)
    t_id = lax.broadcasted_iota(jnp.int32, (CHUNK, GW), 0)
    i_id = jnp.bitwise_and(lax.broadcasted_iota(jnp.int32, (CHUNK, GW), 1), RW_N - 1)
    eye = jnp.where(i_id == t_id, 1.0, 0.0)
    mbd = mbd_ref[...]
    mbd_f2 = jnp.concatenate([mbd, mbd], axis=1).astype(F32)
    zeros = jnp.zeros((CHUNK, GW), F32)

    def bd(m):
        mb = m.astype(BF16)
        return jnp.concatenate([mb] * HGROUP, axis=0) * mbd

    logw_d, k_d_d, kka_d = [], [], []
    for d in range(2):
        w_log = -_softplus(-(w0_ref[d] + _mm(wd, w2_ref[d]))) - 0.5
        logw_d.append(-jnp.exp(w_log))
        a = _sigmoid(a0_ref[d] + _mm(ad, a2_ref[d]))
        k_d_d.append(k * (1.0 + (a - 1.0) * ka_ref[...]))
        kka_d.append(kk * a)
    units = [(d, ci) for d in range(2) for ci in range(RW_TILE // CHUNK)]
    rows_of = lambda ci: slice(ci * CHUNK, (ci + 1) * CHUNK)
    tris = [jnp.where((col >= row) if d == 1 else (col <= row), 1.0, 0.0) for d in range(2)]
    cums = [_mm_exact_lhs(tris[d], logw_d[d][rows_of(ci)]) for d, ci in units]

    chains = []
    for (d, ci), c in zip(units, cums):
        rev = d == 1
        rows = rows_of(ci)
        logw = logw_d[d][rows]
        k_d = k_d_d[d][rows]
        kka = kka_d[d][rows]
        last = 0 if rev else CHUNK - 1
        c_tot = c[last:last + 1, :]
        e_neg = jnp.exp(-c)
        e_rem = jnp.exp(c_tot - c)
        alpha = kk[rows] * jnp.exp(c - logw)
        beta = kka * e_neg
        kappa = k_d * e_neg
        rho = r[rows] * jnp.exp(c)
        kap_hat = k_d * e_rem
        bet_hat = kka * e_rem
        wt_ref[d, 0, ci] = jnp.exp(c_tot)
        strict = (i_id > t_id) if rev else (i_id < t_id)
        nonstrict = (i_id >= t_id) if rev else (i_id <= t_id)
        for g in range(N_GROUPS):
            sl = slice(GW * g, GW * (g + 1))
            chains.append(dict(alpha=alpha[:, sl], beta=beta[:, sl], kappa=kappa[:, sl], rho=rho[:, sl],
                               kap_hat=kap_hat[:, sl], bet_hat=bet_hat[:, sl], v=v[rows, sl],
                               strict=strict, nonstrict=nonstrict))

    ar = [jnp.concatenate([ch["alpha"], ch["rho"]], axis=0) for ch in chains]
    ab = [_mm_nt(x, bd(ch["beta"])) for x, ch in zip(ar, chains)]
    ak = [_mm_nt(x, bd(ch["kappa"])) for x, ch in zip(ar, chains)]
    l_m = [jnp.where(ch["strict"], x[:CHUNK], 0.0) for x, ch in zip(ab, chains)]
    p_b = [jnp.where(ch["nonstrict"], x[CHUNK:], 0.0) for x, ch in zip(ab, chains)]
    akm = [jnp.concatenate([jnp.where(ch["strict"], x[:CHUNK], 0.0),
                            jnp.where(ch["nonstrict"], x[CHUNK:], 0.0)], axis=0) for x, ch in zip(ak, chains)]
    akv = [_mm(x, bd(ch["v"])) for x, ch in zip(akm, chains)]
    t_m = _unit_tri_inverse(l_m, bd, eye, t_id, i_id)
    tu = [_mm(t, jnp.concatenate([bd(x[:CHUNK]), bd(ch["alpha"])], axis=1)) for t, x, ch in zip(t_m, akv, chains)]
    pb = [_mm(p, jnp.concatenate([bd(x[:, :GW]), bd(x[:, GW:])], axis=1)) for p, x in zip(p_b, tu)]
    y0 = [x[CHUNK:] - p[:, :GW] for x, p in zip(akv, pb)]
    rt = [ch["rho"] - p[:, GW:] for ch, p in zip(chains, pb)]
    full = [_mm_tn(jnp.concatenate([ch["v"], x[:, :GW], x[:, GW:]], axis=0),
                   jnp.concatenate(
                       [jnp.concatenate([ch["kap_hat"], -ch["bet_hat"], zeros], axis=0),
                        jnp.concatenate([zeros, zeros, ch["bet_hat"]], axis=0)], axis=1)) * mbd_f2
            for ch, x in zip(chains, tu)]
    blk = [sum(x[RW_N * h:RW_N * (h + 1)] for h in range(1, HGROUP)) + x[0:RW_N] for x in full]

    for ui, (d, ci) in enumerate(units):
        rows = rows_of(ci)
        sel = range(ui * N_GROUPS, (ui + 1) * N_GROUPS)
        y0_ref[d, 0, rows, :] = jnp.concatenate([y0[i] for i in sel], axis=1)
        rt_ref[d, 0, rows, :] = jnp.concatenate([rt[i] for i in sel], axis=1).astype(BF16)
        gm_ref[d, 0, rows, :] = jnp.concatenate([blk[i][:, GW:] for i in sel], axis=1).astype(BF16)
        dm_ref[d, 0, rows, :] = jnp.concatenate([blk[i][:, :GW] for i in sel], axis=1)


def _rwkv_pre(u_rw, prm, ctx_len):
    b, t, _ = u_rw.shape
    n_tiles = t // RW_TILE
    nc = t // CHUNK
    cpt = RW_TILE // CHUNK
    consts = [prm["mu_p"], prm["mu_n"], prm["w0"], prm["w2"], prm["a0"], prm["a2"], prm["kk"], prm["ka"],
              prm["ones_bd"], prm["mask_bd"]]
    tile_out = pl.BlockSpec((2, 1, RW_TILE, RW_W), lambda i, j: (0, i, j, 0))
    return pl.pallas_call(
        functools.partial(_rwkv_pre_kernel, ctx_tiles=ctx_len // RW_TILE, n_tiles=n_tiles),
        grid=(b, n_tiles),
        in_specs=_halo_specs(RW_TILE, t // SUBLANES) + [_const_spec(c.shape) for c in consts],
        out_specs=[tile_out, tile_out, tile_out, tile_out,
                   pl.BlockSpec((2, 1, cpt, 1, RW_W), lambda i, j: (0, i, j, 0, 0))],
        out_shape=[jax.ShapeDtypeStruct((2, b, t, RW_W), F32),
                   jax.ShapeDtypeStruct((2, b, t, RW_W), BF16),
                   jax.ShapeDtypeStruct((2, b, t, RW_W), BF16),
                   jax.ShapeDtypeStruct((2, b, t, RW_W), F32),
                   jax.ShapeDtypeStruct((2, b, nc, 1, RW_W), F32)],
        scratch_shapes=[pltpu.VMEM((RW_TILE + 2 * SUBLANES, RW_COLS), F32)],
        compiler_params=_params(2),
        name="rwkv_pre",
    )(u_rw, u_rw, u_rw, *consts)


def _rwkv_scan_kernel(y0_ref, rt_ref, gm_ref, dm_ref, wt_ref, mbd_ref, y_ref, st_ref, *, nb):
    @pl.when(pl.program_id(2) == 0)
    def _():
        st_ref[...] = jnp.zeros_like(st_ref)

    mbd = mbd_ref[...]

    def bd(mb):
        return jnp.concatenate([mb] * HGROUP, axis=0) * mbd

    for bi in range(nb):
        ys = []
        for g in range(N_GROUPS):
            sl = slice(GW * g, GW * (g + 1))
            s0 = st_ref[bi, g]
            s0b = s0.astype(BF16)
            ys.append(y0_ref[0, bi, :, sl] + lax.dot_general(
                rt_ref[0, bi, :, sl], bd(s0b), (((1,), (1,)), ((), ())), preferred_element_type=F32))
            st_ref[bi, g] = (s0 * wt_ref[0, bi, 0, :, sl]
                             - jnp.dot(s0b, bd(gm_ref[0, bi, :, sl]), preferred_element_type=F32)
                             + dm_ref[0, bi, :, sl])
        y_ref[0, bi] = jnp.concatenate(ys, axis=1)


def _rwkv_scan(y0, rt, gm, dm, wt, mask_bd, nc_ctx):
    _, b, t, _ = y0.shape
    nc_tot = t // CHUNK
    nb = RW_SCAN_BATCH if b % RW_SCAN_BATCH == 0 else 1

    def cidx(d, s):
        rev = jnp.where(s < nc_ctx, nc_ctx - 1 - s, nc_tot - 1 - (s - nc_ctx))
        return jnp.where(d == 1, rev, s)

    chunk = pl.BlockSpec((1, nb, CHUNK, RW_W), lambda d, i, s: (d, i, cidx(d, s), 0))
    return pl.pallas_call(
        functools.partial(_rwkv_scan_kernel, nb=nb),
        grid=(2, b // nb, nc_tot),
        in_specs=[chunk, chunk, chunk, chunk,
                  pl.BlockSpec((1, nb, 1, 1, RW_W), lambda d, i, s: (d, i, cidx(d, s), 0, 0)),
                  _const_spec(mask_bd.shape)],
        out_specs=chunk,
        out_shape=jax.ShapeDtypeStruct(y0.shape, F32),
        scratch_shapes=[pltpu.VMEM((nb, N_GROUPS, RW_N, GW), F32)],
        compiler_params=_params(3, arbitrary_last=True),
        name="rwkv_scan",
    )(y0, rt, gm, dm, wt, mask_bd)


def _rwkv_out_kernel(u_ref, up_ref, un_ref, y_ref, mup_ref, mun_ref, rk_ref, g2_ref, lnw_ref, lnb_ref, ones_ref,
                     out_ref, buf_ref, *, ctx_tiles, n_tiles):
    j = pl.program_id(1)
    is_first = jnp.logical_or(j == 0, j == ctx_tiles)
    is_last = jnp.logical_or(j == ctx_tiles - 1, j == n_tiles - 1)
    w = RW_W
    rkv, gd = _shifted_tile(u_ref, up_ref, un_ref, buf_ref, mup_ref, mun_ref, is_first, is_last, TILE,
                            [(0, 3 * w), (RW_COLS - LORA_GATE, RW_COLS)])
    r = rkv[:, 0:w]
    k = rkv[:, w:2 * w]
    v = rkv[:, 2 * w:3 * w]
    ones_bd = ones_ref[...]
    y = y_ref[0, 0] + y_ref[1, 0]
    mean = _segsum(y, ones_bd) * (1.0 / RW_N)
    yc = y - mean
    var = _segsum(yc * yc, ones_bd) * (1.0 / RW_N)
    yn = yc * lax.rsqrt(var + RW_GN_EPS) * lnw_ref[...] + lnb_ref[...]
    bonus = _segsum(r * k * rk_ref[...], ones_bd) * v
    out_ref[0] = (yn + bonus) * _mm(_sigmoid(gd), g2_ref[...])


def _rwkv_out(u_rw, y, prm, ctx_len):
    b, t, _ = u_rw.shape
    n_tiles = t // TILE
    consts = [prm["mu_p"], prm["mu_n"], prm["rk"], prm["g2"], prm["lnw"], prm["lnb"], prm["ones_bd"]]
    return pl.pallas_call(
        functools.partial(_rwkv_out_kernel, ctx_tiles=ctx_len // TILE, n_tiles=n_tiles),
        grid=(b, n_tiles),
        in_specs=_halo_specs(TILE, t // SUBLANES)
        + [pl.BlockSpec((2, 1, TILE, RW_W), lambda i, j: (0, i, j, 0))]
        + [_const_spec(c.shape) for c in consts],
        out_specs=_tile_spec(RW_W),
        out_shape=jax.ShapeDtypeStruct((b, t, RW_W), F32),
        scratch_shapes=[pltpu.VMEM((TILE + 2 * SUBLANES, RW_COLS), F32)],
        compiler_params=_params(2),
        name="rwkv_out",
    )(u_rw, u_rw, u_rw, y, *consts)


def _rwkv_mixer(u_rw, prm, ctx_len):
    y0, rt, gm, dm, wt = _rwkv_pre(u_rw, prm, ctx_len)
    y = _rwkv_scan(y0, rt, gm, dm, wt, prm["mask_bd"], ctx_len // CHUNK)
    return _rwkv_out(u_rw, y, prm, ctx_len)


ML_BATCH = 8
ML_CHUNK = 128


def _mlstm_kernel(u_ref, g_ref, bias_ref, out_ref, c_ref, n_ref, m_ref, *, rev, d, nb):
    s = pl.program_id(1)

    @pl.when(s == 0)
    def _():
        c_ref[...] = jnp.zeros_like(c_ref)
        n_ref[...] = jnp.zeros_like(n_ref)
        m_ref[...] = jnp.zeros_like(m_ref)

    lane = lax.broadcasted_iota(jnp.int32, (ML_CHUNK, MG_COLS), 1)
    is_forget = jnp.bitwise_and(lane, ML_H) != 0
    row = lax.broadcasted_iota(jnp.int32, (ML_CHUNK, ML_CHUNK), 0)
    col = lax.broadcasted_iota(jnp.int32, (ML_CHUNK, ML_CHUNK), 1)
    seen = (col >= row) if rev else (col <= row)
    tri = jnp.where(seen, 1.0, 0.0)
    last = 0 if rev else ML_CHUNK - 1

    us = [u_ref[bi] for bi in range(nb)]
    gps = [g_ref[bi] + bias_ref[...] for bi in range(nb)]
    gates = [jnp.where(is_forget, -_softplus(-gp), gp) for gp in gps]
    gates_t = [g.T for g in gates]
    cum_col = [_mm_exact_lhs(tri, g) for g in gates]
    cum_row = [_mm_exact_rhs_nt(g, tri) for g in gates_t]

    chains = [(bi, h) for bi in range(nb) for h in range(ML_H)]
    jf = lambda h: 2 * ML_H * d + ML_H + h
    ji = lambda h: 2 * ML_H * d + h
    q = [us[bi][:, ML_DH * h:ML_DH * (h + 1)] for bi, h in chains]
    k = [us[bi][:, ML_W + ML_DH * h:ML_W + ML_DH * (h + 1)] * (ML_DH ** -0.5) for bi, h in chains]
    v = [us[bi][:, 2 * ML_W + ML_DH * h:2 * ML_W + ML_DH * (h + 1)] for bi, h in chains]
    b_col = [cum_col[bi][:, jf(h):jf(h) + 1] for bi, h in chains]
    b_row = [cum_row[bi][jf(h):jf(h) + 1, :] for bi, h in chains]
    i_col = [gates[bi][:, ji(h):ji(h) + 1] for bi, h in chains]
    i_row = [gates_t[bi][ji(h):ji(h) + 1, :] for bi, h in chains]
    c_st = [c_ref[bi * ML_H + h] for bi, h in chains]
    n_st = [n_ref[bi * ML_H + h] for bi, h in chains]
    m_st = [m_ref[bi * ML_H + h][:, 0:1] for bi, h in chains]
    n_ch = range(len(chains))

    qk = [_mm_nt(q[i], k[i]) for i in n_ch]
    qc = [_mm_nt(q[i], c_st[i]) for i in n_ch]
    dlog = [jnp.where(seen, b_col[i] - b_row[i] + i_row[i], -jnp.inf) for i in n_ch]
    inter = [b_col[i] + m_st[i] for i in n_ch]
    m_t = [jnp.maximum(inter[i], jnp.max(dlog[i], axis=1, keepdims=True)) for i in n_ch]
    s_mat = [qk[i] * jnp.exp(dlog[i] - m_t[i]) for i in n_ch]
    sv = [_mm(s_mat[i], v[i]) for i in n_ch]
    b_last = [b_col[i][last:last + 1, :] for i in n_ch]
    wlog = [b_last[i] - b_col[i] + i_col[i] for i in n_ch]
    m_new = [jnp.maximum(b_last[i] + m_st[i], jnp.max(wlog[i], axis=0, keepdims=True)) for i in n_ch]
    wk = [jnp.exp(wlog[i] - m_new[i]) * k[i] for i in n_ch]
    vwk = [_mm_tn(v[i], wk[i]) for i in n_ch]

    hs = []
    for i, (bi, h) in enumerate(chains):
        w_inter = jnp.exp(inter[i] - m_t[i])
        num = w_inter * qc[i] + sv[i]
        den = (w_inter * jnp.sum(q[i] * n_st[i], axis=1, keepdims=True)
               + jnp.sum(s_mat[i], axis=1, keepdims=True))
        hs.append(num / jnp.maximum(jnp.abs(den), jnp.exp(-m_t[i])))
        carry_w = jnp.exp(b_last[i] + m_st[i] - m_new[i])
        st = bi * ML_H + h
        c_ref[st] = carry_w * c_st[i] + vwk[i]
        n_ref[st] = carry_w * n_st[i] + jnp.sum(wk[i], axis=0, keepdims=True)
        m_ref[st] = jnp.broadcast_to(m_new[i], (1, LANES))

    for bi in range(nb):
        out_ref[bi] = jnp.concatenate(hs[bi * ML_H:(bi + 1) * ML_H], axis=1)


def _mlstm_pass(u_ml, u_mg, gate_bias, rev, ctx_len):
    b, t, _ = u_ml.shape
    nc_tot = t // ML_CHUNK
    nb = ML_BATCH if b % ML_BATCH == 0 else 1
    cidx = functools.partial(_chunk_index, rev=rev, nc_ctx=ctx_len // ML_CHUNK, nc_tot=nc_tot)
    in_specs = [pl.BlockSpec((nb, ML_CHUNK, 3 * ML_W), lambda i, s: (i, cidx(s), 0)),
                pl.BlockSpec((nb, ML_CHUNK, MG_COLS), lambda i, s: (i, cidx(s), 0)),
                _const_spec(gate_bias.shape)]
    return pl.pallas_call(
        functools.partial(_mlstm_kernel, rev=rev, d=1 if rev else 0, nb=nb),
        grid=(b // nb, nc_tot),
        in_specs=in_specs,
        out_specs=pl.BlockSpec((nb, ML_CHUNK, ML_W), lambda i, s: (i, cidx(s), 0)),
        out_shape=jax.ShapeDtypeStruct((b, t, ML_W), F32),
        scratch_shapes=[pltpu.VMEM((nb * ML_H, ML_DH, ML_DH), F32),
                        pltpu.VMEM((nb * ML_H, 1, ML_DH), F32),
                        pltpu.VMEM((nb * ML_H, 1, LANES), F32)],
        compiler_params=_params(2, arbitrary_last=True),
        name="mlstm_bwd" if rev else "mlstm_fwd",
    )(u_ml, u_mg, gate_bias)


def _conv_kernel(x_ref, mod_ref, g_ref, w_ref, wrec_ref, dw_ref, db_ref, lnw_ref, lnb_ref,
                 urw_ref, uml_ref, umg_ref, out_ref, *, n_ctx_tiles):
    d = D_MODEL
    mod = mod_ref[0]
    h = _norm_mod(x_ref[0], g_ref[...], mod[:, 0:d], mod[:, d:2 * d]).astype(BF16)
    u = _mm(h, w_ref[...])
    z = u[:, :CONV_W] * _sigmoid(u[:, CONV_W:])
    dw = dw_ref[...]
    half = CONV_K // 2

    n_blk = TILE // GRID_W
    rec_cols = wrec_ref.shape[1]
    outs = [(0, RW_COLS, urw_ref), (RW_COLS, RW_COLS + 4 * ML_W, uml_ref), (RW_COLS + 4 * ML_W, rec_cols, umg_ref)]

    def slab(part):
        return part * rec_cols // n_blk, (part + 1) * rec_cols // n_blk

    def project(part):
        c0, c1 = slab(part)
        return _mm(h, wrec_ref[:, c0:c1])

    def store_slab(part, u_rec):
        c0, c1 = slab(part)
        for o0, o1, ref in outs:
            a, b = max(c0, o0), min(c1, o1)
            if a < b:
                ref[0, :, a - o0:b - o0] = u_rec[:, a - c0:b - c0]

    def conv_segments(seg_len):
        lead = CONV_PAD - half
        win_rows = GRID_W + 2 * CONV_PAD
        for blk in range(n_blk):
            u_rec = project(blk)
            row0 = blk * GRID_W
            seg_start = row0 // seg_len * seg_len
            lo, hi = row0 - CONV_PAD, row0 + GRID_W + CONV_PAD
            parts = [z[max(lo, seg_start):min(hi, seg_start + seg_len)]]
            if lo < seg_start:
                parts.insert(0, jnp.zeros((seg_start - lo, CONV_W), F32))
            if hi > seg_start + seg_len:
                parts.append(jnp.zeros((hi - seg_start - seg_len, CONV_W), F32))
            win = jnp.concatenate(parts, axis=0) if len(parts) > 1 else parts[0]
            acc = jnp.zeros((GRID_W, CONV_W), F32)
            for res in range(SUBLANES):
                rot = win if res == 0 else pltpu.roll(win, win_rows - res, 0)
                for q in range((lead + CONV_K - 1 - res) // SUBLANES + 1):
                    j = SUBLANES * q + res - lead
                    if 0 <= j < CONV_K:
                        acc = acc + dw[j:j + 1, :] * rot[SUBLANES * q:SUBLANES * q + GRID_W]
            y = _layernorm_lanes(acc + db_ref[...], LN_EPS) * lnw_ref[...] + lnb_ref[...]
            out_ref[0, row0:row0 + GRID_W, :] = y * _sigmoid(y)
            store_slab(blk, u_rec)

    is_ctx = pl.program_id(1) < n_ctx_tiles

    @pl.when(is_ctx)
    def _():
        conv_segments(TILE)

    @pl.when(jnp.logical_not(is_ctx))
    def _():
        conv_segments(GRID_W)


def _proj_conv(xa, mod, g, w_rec, prm, ctx_row, n_ctx_tiles):
    b, t, d = xa.shape
    consts = [g, prm["w_cv"], w_rec, prm["dw"], prm["db"], prm["lnw"], prm["lnb"]]
    widths = [RW_COLS, 4 * ML_W, MG_COLS, CONV_W]
    return pl.pallas_call(
        functools.partial(_conv_kernel, n_ctx_tiles=n_ctx_tiles),
        grid=(b, t // TILE),
        in_specs=[_tile_spec(d), _mod_spec(ctx_row, n_ctx_tiles)] + [_const_spec(c.shape) for c in consts],
        out_specs=[_tile_spec(w) for w in widths],
        out_shape=[jax.ShapeDtypeStruct((b, t, w), F32) for w in widths],
        compiler_params=_params(2),
        name="proj_conv",
    )(xa, mod, *consts)


def _merge_kernel(x_ref, mod_ref, g_ref, ya_ref, yb_ref, hf_ref, hb_ref, o_ref, nw_ref,
                  wg_ref, pa_ref, pb_ref, pc_ref, wo_ref, out_ref):
    d = D_MODEL
    mod = mod_ref[0]
    x = x_ref[0]
    h = _norm_mod(x, g_ref[...], mod[:, 0:d], mod[:, d:2 * d])
    ug = _sigmoid(_mm(h, wg_ref[...]))
    hsum = hf_ref[0] + hb_ref[0]
    hn = jnp.concatenate([_layernorm_lanes(hsum[:, ML_DH * i:ML_DH * (i + 1)], LN_EPS) for i in range(ML_H)],
                         axis=1)
    yc = hn * nw_ref[...] * _sigmoid(o_ref[0])
    m = (ug[:, 0:d] * _mm(ya_ref[0], pa_ref[...])
         + ug[:, d:2 * d] * _mm(yb_ref[0], pb_ref[...])
         + ug[:, 2 * d:3 * d] * _mm(yc, pc_ref[...]))
    out_ref[0] = x + mod[:, 2 * d:3 * d] * _mm(m, wo_ref[...])


def _merge(xa, mod, g, ya, yb, h_f, h_b, u_ml, nw, prm, ctx_row, n_ctx_tiles):
    b, t, d = xa.shape
    consts = [prm["w_gate"], prm["p_a"], prm["p_b"], prm["p_c"], prm["w_out"]]
    o_cols = 3 * ML_W // ML_W
    return pl.pallas_call(
        _merge_kernel,
        grid=(b, t // TILE),
        in_specs=[_tile_spec(d), _mod_spec(ctx_row, n_ctx_tiles), _const_spec(g.shape),
                  _tile_spec(RW_W), _tile_spec(CONV_W), _tile_spec(ML_W), _tile_spec(ML_W),
                  pl.BlockSpec((1, TILE, ML_W), lambda i, j: (i, j, o_cols)), _const_spec(nw.shape)]
        + [_const_spec(c.shape) for c in consts],
        out_specs=_tile_spec(d),
        out_shape=jax.ShapeDtypeStruct((b, t, d), F32),
        compiler_params=_params(2),
        name="merge",
    )(xa, mod, g, ya, yb, h_f, h_b, u_ml, nw, *consts)


FF_CHUNK = 1024


def _mlp_kernel(x_ref, mod_ref, g_ref, w1_ref, w2_ref, out_ref):
    d = D_MODEL
    mod = mod_ref[0]
    x = x_ref[0]
    h = _norm_mod(x, g_ref[...], mod[:, 3 * d:4 * d], mod[:, 4 * d:5 * d]).astype(BF16)
    acc = jnp.zeros((TILE, d), F32)
    for i in range(D_FF // FF_CHUNK):
        hid = jnp.maximum(_mm(h, w1_ref[:, i * FF_CHUNK:(i + 1) * FF_CHUNK]), 0.0)
        acc = acc + _mm(hid * hid, w2_ref[i * FF_CHUNK:(i + 1) * FF_CHUNK, :])
    out_ref[0] = x + mod[:, 5 * d:6 * d] * acc


def _mlp(xa, mod, g, w1, w2, ctx_row, n_ctx_tiles):
    b, t, d = xa.shape
    return pl.pallas_call(
        _mlp_kernel,
        grid=(b, t // TILE),
        in_specs=[_tile_spec(d), _mod_spec(ctx_row, n_ctx_tiles), _const_spec(g.shape),
                  _const_spec(w1.shape), _const_spec(w2.shape)],
        out_specs=_tile_spec(d),
        out_shape=jax.ShapeDtypeStruct((b, t, d), F32),
        compiler_params=_params(2),
        name="mlp",
    )(xa, mod, g, w1, w2)


def _final_kernel(x_ref, g_ref, out_ref):
    x = x_ref[0]
    out_ref[0] = x * lax.rsqrt(jnp.mean(x * x, axis=-1, keepdims=True) + NORM_EPS) * g_ref[...]


def _final_norm(xa, g, n_ctx_tiles):
    b, t, d = xa.shape
    n_lat = t // TILE - n_ctx_tiles
    return pl.pallas_call(
        _final_kernel,
        grid=(b, n_lat),
        in_specs=[pl.BlockSpec((1, TILE, d), lambda i, j: (i, j + n_ctx_tiles, 0)), _const_spec(g.shape)],
        out_specs=pl.BlockSpec((1, TILE, d), lambda i, j: (i, j, 0)),
        out_shape=jax.ShapeDtypeStruct((b, n_lat * TILE, d), F32),
        compiler_params=_params(2),
        name="final_norm",
    )(xa, g)


def _pad_rows(m, rows, offset):
    out = jnp.zeros((rows, m.shape[1]), m.dtype)
    return out.at[offset:offset + m.shape[0]].set(m)


def kernel(x, c, ctx, c_ctx, w_ada, b_ada, g_norm1, g_norm2, w_in, mu_prev, mu_next,
           rw_w0, rw_w2, rw_a0, rw_a2, rw_kk, rw_ka, rw_rk, rw_g2, rw_lnw, rw_lnb,
           cv_dw, cv_db, cv_lnw, cv_lnb, ml_ib, ml_fb, ml_nw,
           p_a, p_b, p_c, w_out, w_mlp1, w_mlp2, g_final):
    batch, seq, d = x.shape
    ctx_len = ctx.shape[1]
    depth = w_ada.shape[0]
    assert d == D_MODEL and ctx_len == TILE and seq % TILE == 0
    n_ctx_tiles = ctx_len // TILE

    ctx_row = batch
    rows = -(-(batch + 1) // SUBLANES) * SUBLANES
    c_all = jnp.zeros((rows, d), F32).at[:batch].set(c).at[ctx_row].set(c_ctx)
    mods = _ada_mods(c_all, w_ada, b_ada)

    lane = jnp.arange(GW)
    mask_bd = (lane[:, None] // RW_N == lane[None, :] // RW_N).astype(BF16)
    ones_bd = mask_bd

    xa = jnp.concatenate([ctx, x], axis=1)
    for l in range(depth):
        mod = mods[l].reshape(rows, 1, N_MOD * d)
        g1 = g_norm1[l].reshape(1, d)
        g2 = g_norm2[l].reshape(1, d)
        wl = w_in[l]
        gate_w = _pad_rows(wl[:, OFF_ML + 4 * ML_W:OFF_GATE].T, MG_COLS, 0).T
        w_rec = jnp.concatenate([wl[:, :RW_COLS], wl[:, OFF_ML:OFF_ML + 4 * ML_W], gate_w],
                                axis=1).astype(BF16)
        cv = {"w_cv": wl[:, OFF_CV:OFF_ML].astype(BF16), "dw": cv_dw[l],
              "db": cv_db[l].reshape(1, CONV_W), "lnw": cv_lnw[l].reshape(1, CONV_W),
              "lnb": cv_lnb[l].reshape(1, CONV_W)}
        u_rw, u_ml, u_mg, yb = _proj_conv(xa, mod, g1, w_rec, cv, ctx_row, n_ctx_tiles)

        rw = {
            "mu_p": mu_prev[l].reshape(1, RW_COLS), "mu_n": mu_next[l].reshape(1, RW_COLS),
            "w0": rw_w0[l].reshape(2, 1, RW_W), "a0": rw_a0[l].reshape(2, 1, RW_W),
            "w2": jnp.stack([_pad_rows(rw_w2[l, i], 2 * LORA_DECAY, LORA_DECAY * i)
                             for i in range(2)]).astype(BF16),
            "a2": jnp.stack([_pad_rows(rw_a2[l, i], 2 * LORA_ICLR, LORA_ICLR * i)
                             for i in range(2)]).astype(BF16),
            "kk": rw_kk[l].reshape(1, RW_W), "ka": rw_ka[l].reshape(1, RW_W),
            "rk": rw_rk[l].reshape(1, RW_W), "g2": rw_g2[l].astype(BF16),
            "lnw": rw_lnw[l].reshape(1, RW_W), "lnb": rw_lnb[l].reshape(1, RW_W),
            "ones_bd": ones_bd, "mask_bd": mask_bd,
        }
        ya = _rwkv_mixer(u_rw, rw, ctx_len)

        gate_bias = jnp.zeros((1, MG_COLS), F32).at[0, :4 * ML_H].set(
            jnp.stack([ml_ib[l], ml_fb[l]], axis=1).reshape(-1))
        h_b = _mlstm_pass(u_ml, u_mg, gate_bias, True, ctx_len)
        h_f = _mlstm_pass(u_ml, u_mg, gate_bias, False, ctx_len)

        mg = {"w_gate": wl[:, OFF_GATE:].astype(BF16), "p_a": p_a[l].astype(BF16),
              "p_b": p_b[l].astype(BF16), "p_c": p_c[l].astype(BF16), "w_out": w_out[l].astype(BF16)}
        xa = _merge(xa, mod, g1, ya, yb, h_f, h_b, u_ml, ml_nw[l].reshape(1, ML_W), mg, ctx_row, n_ctx_tiles)
        xa = _mlp(xa, mod, g2, w_mlp1[l].astype(BF16), w_mlp2[l].astype(BF16), ctx_row, n_ctx_tiles)
    return _final_norm(xa, g_final.reshape(1, d), n_ctx_tiles)
```
